```python
import math
import jax, jax.numpy as jnp
from jax import lax
import numpy as np

D_MODEL = 1024
BATCH = 16
SEQ = 4096
DEPTH = 4

HEAD_DIM = 64
A_HEADS = 8
A_QK = A_HEADS * 2 * HEAD_DIM
A_V = A_HEADS * 2 * HEAD_DIM
B_HEADS = 12
B_W = B_HEADS * HEAD_DIM
C_CH = 768
CONV_W = 31
GATE_W = 3 * D_MODEL
D_FF = 2816
IN_SIZES = (A_QK, A_QK, A_V, B_W, B_W, B_W, 2 * C_CH, GATE_W)
IN_SPLITS = tuple(int(v) for v in np.cumsum(IN_SIZES)[:-1])
IN_W = int(sum(IN_SIZES))
Q_BLK = 128
DIL_BLK = 64
DILATED_PATTERNS = ((128, 1), (512, 4), (2048, 16))
ATTN_SCALE = HEAD_DIM ** -0.5
EPS = 1e-6

kernel_name = 'hybrid_diffattn_dilated_conformer_encoder'


def rms_norm(x, g):
    xf = x.astype(jnp.float32)
    y = xf * lax.rsqrt(jnp.mean(xf * xf, axis=-1, keepdims=True) + EPS)
    return (y * g.astype(jnp.float32)).astype(x.dtype)


def alibi_slopes(n):
    return jnp.asarray(2.0 ** (-8.0 * np.arange(1, n + 1) / n), dtype=jnp.float32)


def lambda_init(layer):
    return 0.8 - 0.6 * math.exp(-0.3 * layer)


def swiglu(h, w_up, w_down):
    a, b = jnp.split(h @ w_up, 2, axis=-1)
    return (jax.nn.silu(a) * b) @ w_down


def diff_attention(q, k, v, lam, slopes):
    bn, s_len, h, _, dh = q.shape
    nblk = s_len // Q_BLK
    qb = q.reshape(bn, nblk, Q_BLK, h, 2, dh).transpose(1, 0, 3, 4, 2, 5)
    kt = k.transpose(0, 2, 3, 1, 4)
    vt = v.transpose(0, 2, 1, 3)
    pos = jnp.arange(s_len)

    def one_block(args):
        qblk, start = args
        sc = jnp.einsum('bhiqd,bhikd->bhiqk', qblk, kt).astype(jnp.float32) * ATTN_SCALE
        tq = start + jnp.arange(Q_BLK)
        dist = jnp.abs(tq[:, None] - pos[None, :]).astype(jnp.float32)
        sc = sc - slopes[None, :, None, None, None] * dist
        p = jax.nn.softmax(sc, axis=-1)
        w = p[:, :, 0] - lam * p[:, :, 1]
        return jnp.einsum('bhqk,bhkd->bhqd', w.astype(v.dtype), vt)

    o = lax.map(one_block, (qb, jnp.arange(nblk) * Q_BLK))
    return o.transpose(1, 0, 3, 2, 4).reshape(bn, s_len, h, 2 * dh)


def dilated_pattern(q, k, v, slopes, dil, radius):
    bn, s_len, h, dh = q.shape
    u_len = s_len // dil

    def to_res(t):
        return t.reshape(bn, u_len, dil, h, dh).transpose(0, 2, 3, 1, 4)

    qr, kr, vr = to_res(q), to_res(k), to_res(v)
    nb = -(-u_len // DIL_BLK)
    up = nb * DIL_BLK
    pad = up - u_len
    qb = jnp.pad(qr, ((0, 0), (0, 0), (0, 0), (0, pad), (0, 0))).reshape(bn, dil, h, nb, DIL_BLK, dh)

    def windows(t):
        tp = jnp.pad(t, ((0, 0), (0, 0), (0, 0), (DIL_BLK, pad + DIL_BLK), (0, 0)))
        tb = tp.reshape(bn, dil, h, nb + 2, DIL_BLK, dh)
        return jnp.concatenate([tb[:, :, :, :-2], tb[:, :, :, 1:-1], tb[:, :, :, 2:]], axis=4)

    kw, vw = windows(kr), windows(vr)
    sc = jnp.einsum('brhnqd,brhnkd->brhnqk', qb, kw).astype(jnp.float32) * ATTN_SCALE
    uq = jnp.arange(nb)[:, None] * DIL_BLK + jnp.arange(DIL_BLK)[None, :]
    uk = jnp.arange(nb)[:, None] * DIL_BLK - DIL_BLK + jnp.arange(3 * DIL_BLK)[None, :]
    du = jnp.abs(uq[:, :, None] - uk[:, None, :])
    valid = (du <= radius) & (uk[:, None, :] >= 0) & (uk[:, None, :] < u_len)
    bias = -slopes[:, None, None, None] * (du * dil).astype(jnp.float32)
    sc = jnp.where(valid, sc + bias, -jnp.inf)
    m = jnp.max(sc, axis=-1, keepdims=True)
    p = jnp.exp(sc - m)
    den = jnp.sum(p, axis=-1)
    lse = m[..., 0] + jnp.log(den)
    o = jnp.einsum('brhnqk,brhnkd->brhnqd', p.astype(v.dtype), vw).astype(jnp.float32) / den[..., None]
    o = o.reshape(bn, dil, h, up, dh)[:, :, :, :u_len].transpose(0, 3, 1, 2, 4).reshape(bn, s_len, h, dh)
    lse = lse.reshape(bn, dil, h, up)[:, :, :, :u_len].transpose(0, 3, 1, 2).reshape(bn, s_len, h)
    return o, lse


def dilated_mixture(q, k, v, slopes):
    outs, lses = [], []
    for window, dil in DILATED_PATTERNS:
        o, lse = dilated_pattern(q, k, v, slopes, dil, window // 2 // dil)
        outs.append(o)
        lses.append(lse)
    wts = jax.nn.softmax(jnp.stack(lses, 0), axis=0)
    o = jnp.einsum('gbsh,gbshd->bshd', wts, jnp.stack(outs, 0))
    return o.astype(q.dtype)


def conv_module(u, dw_w, dw_b, norm_g):
    a, gate = jnp.split(u, 2, axis=-1)
    z = a * jax.nn.sigmoid(gate)
    z = lax.conv_general_dilated(
        z, dw_w[:, None, :].astype(z.dtype), window_strides=(1,),
        padding=[(CONV_W // 2, CONV_W // 2)],
        dimension_numbers=('NWC', 'WIO', 'NWC'),
        feature_group_count=C_CH) + dw_b
    return jax.nn.silu(rms_norm(z, norm_g))


def setup_inputs(seed: int = 0) -> dict:
    key = jax.random.key(seed)
    ks = iter(jax.random.split(key, 32))
    L = DEPTH

    def nrm(shape, scale):
        return jax.random.normal(next(ks), shape, jnp.float32) * scale

    def gain(shape):
        return 1.0 + nrm(shape, 0.02)

    return {
        'x': nrm((BATCH, SEQ, D_MODEL), 1.0),
        'ffn1_norm': gain((L, D_MODEL)),
        'ffn1_w_up': nrm((L, D_MODEL, 2 * D_FF), D_MODEL ** -0.5),
        'ffn1_w_down': nrm((L, D_FF, D_MODEL), D_FF ** -0.5),
        'mix_norm': gain((L, D_MODEL)),
        'w_in': nrm((L, D_MODEL, IN_W), D_MODEL ** -0.5),
        'b_in': nrm((L, IN_W), 0.02),
        'a_q_norm': gain((L, HEAD_DIM)),
        'a_k_norm': gain((L, HEAD_DIM)),
        'a_lambda': nrm((L, 4, HEAD_DIM), 0.1),
        'a_sub_norm': gain((L, 2 * HEAD_DIM)),
        'w_out_a': nrm((L, A_V, D_MODEL), A_V ** -0.5),
        'b_q_norm': gain((L, HEAD_DIM)),
        'b_k_norm': gain((L, HEAD_DIM)),
        'w_out_b': nrm((L, B_W, D_MODEL), B_W ** -0.5),
        'c_dw_w': nrm((L, CONV_W, C_CH), CONV_W ** -0.5),
        'c_dw_b': nrm((L, C_CH), 0.02),
        'c_norm': gain((L, C_CH)),
        'w_out_c': nrm((L, C_CH, D_MODEL), C_CH ** -0.5),
        'w_out': nrm((L, D_MODEL, D_MODEL), D_MODEL ** -0.5),
        'ffn2_norm': gain((L, D_MODEL)),
        'ffn2_w_up': nrm((L, D_MODEL, 2 * D_FF), D_MODEL ** -0.5),
        'ffn2_w_down': nrm((L, D_FF, D_MODEL), D_FF ** -0.5),
    }


def reference(x, ffn1_norm, ffn1_w_up, ffn1_w_down, mix_norm, w_in, b_in,
              a_q_norm, a_k_norm, a_lambda, a_sub_norm, w_out_a,
              b_q_norm, b_k_norm, w_out_b,
              c_dw_w, c_dw_b, c_norm, w_out_c, w_out,
              ffn2_norm, ffn2_w_up, ffn2_w_down):
    bn, s_len, _ = x.shape
    slopes_a = alibi_slopes(A_HEADS)
    slopes_b = alibi_slopes(B_HEADS)
    for l in range(DEPTH):
        x = x + 0.5 * swiglu(rms_norm(x, ffn1_norm[l]), ffn1_w_up[l], ffn1_w_down[l])

        h = rms_norm(x, mix_norm[l])
        proj = h @ w_in[l] + b_in[l]
        aq, ak, av, bq, bk, bv, cu, gl = jnp.split(proj, IN_SPLITS, axis=-1)

        aq = rms_norm(aq.reshape(bn, s_len, A_HEADS, 2, HEAD_DIM), a_q_norm[l])
        ak = rms_norm(ak.reshape(bn, s_len, A_HEADS, 2, HEAD_DIM), a_k_norm[l])
        av = av.reshape(bn, s_len, A_HEADS, 2 * HEAD_DIM)
        lam0 = lambda_init(l)
        lp = a_lambda[l].astype(jnp.float32)
        lam = jnp.exp(jnp.sum(lp[0] * lp[1])) - jnp.exp(jnp.sum(lp[2] * lp[3])) + lam0
        ya = diff_attention(aq, ak, av, lam, slopes_a)
        ya = rms_norm(ya, a_sub_norm[l]) * (1.0 - lam0)
        ya = ya.reshape(bn, s_len, A_V) @ w_out_a[l]

        bq = rms_norm(bq.reshape(bn, s_len, B_HEADS, HEAD_DIM), b_q_norm[l])
        bk = rms_norm(bk.reshape(bn, s_len, B_HEADS, HEAD_DIM), b_k_norm[l])
        bv = bv.reshape(bn, s_len, B_HEADS, HEAD_DIM)
        yb = dilated_mixture(bq, bk, bv, slopes_b).reshape(bn, s_len, B_W) @ w_out_b[l]

        yc = conv_module(cu, c_dw_w[l], c_dw_b[l], c_norm[l]) @ w_out_c[l]

        g = jax.nn.sigmoid(gl).reshape(bn, s_len, 3, D_MODEL)
        merged = g[:, :, 0] * ya + g[:, :, 1] * yb + g[:, :, 2] * yc
        x = x + merged @ w_out[l]

        x = x + 0.5 * swiglu(rms_norm(x, ffn2_norm[l]), ffn2_w_up[l], ffn2_w_down[l])
    return x
```

```python
import functools
import math

import jax
import jax.numpy as jnp
import numpy as np
from jax import lax
from jax.experimental import pallas as pl
from jax.experimental.pallas import tpu as pltpu

HEAD_DIM = 64
LANES = 128
SUBLANES = 8
A_HEADS = 8
B_HEADS = 12
CONV_W = 31
CONV_HALO = 16
DIL_RADIUS = 64
DILATED_PATTERNS = ((128, 1), (512, 4), (2048, 16))
ATTN_SCALE = HEAD_DIM ** -0.5
EPS = 1e-6
NEG_BIG = -1e30
VMEM_LIMIT = 56 * 1024 * 1024

BF16 = jnp.bfloat16
F32 = jnp.float32


def _params(*sem):
    return pltpu.CompilerParams(dimension_semantics=sem, vmem_limit_bytes=VMEM_LIMIT)


def _resident(shape):
    nd = len(shape)
    return pl.BlockSpec(shape, lambda *_: (0,) * nd, pipeline_mode=pl.Buffered(1))


def _rms(xf, g):
    ms = jnp.mean(xf * xf, axis=-1, keepdims=True)
    return xf * lax.rsqrt(ms + EPS) * g


def _dot(a, b):
    return jnp.dot(a, b, preferred_element_type=F32)


def _dot_nt(a, b):
    return lax.dot_general(a, b, (((1,), (1,)), ((), ())), preferred_element_type=F32)


def _ffn_kernel(x_ref, g_ref, wu_ref, wd_ref, o_ref, *, d_ff, n_chunks):
    x = x_ref[...]
    h = _rms(x, g_ref[...]).astype(BF16)
    ck = d_ff // n_chunks
    y = jnp.zeros_like(x)
    for c in range(n_chunks):
        a = _dot(h, wu_ref[:, c * ck:(c + 1) * ck])
        b = _dot(h, wu_ref[:, d_ff + c * ck:d_ff + (c + 1) * ck])
        act = (a * jax.nn.sigmoid(a) * b).astype(BF16)
        y = y + _dot(act, wd_ref[c * ck:(c + 1) * ck, :])
    o_ref[...] = x + 0.5 * y


def _ffn(x2, g, w_up, w_down, *, tm=512):
    t, d = x2.shape
    d_ff = w_down.shape[0]
    return pl.pallas_call(
        functools.partial(_ffn_kernel, d_ff=d_ff, n_chunks=2),
        out_shape=jax.ShapeDtypeStruct((t, d), F32),
        grid=(t // tm,),
        in_specs=[pl.BlockSpec((tm, d), lambda i: (i, 0)),
                  _resident((1, d)), _resident(w_up.shape), _resident(w_down.shape)],
        out_specs=pl.BlockSpec((tm, d), lambda i: (i, 0)),
        compiler_params=_params("parallel"),
        name="ffn",
    )(x2, g, w_up, w_down)


def _headnorm_store(y, g128, o_ref):
    lane = lax.broadcasted_iota(jnp.int32, (1, LANES), 1)
    lo = lane < HEAD_DIM
    for c in range(y.shape[1] // LANES):
        blk = y[:, c * LANES:(c + 1) * LANES]
        sq = blk * blk
        s_lo = jnp.sum(jnp.where(lo, sq, 0.0), axis=-1, keepdims=True)
        s_hi = jnp.sum(jnp.where(lo, 0.0, sq), axis=-1, keepdims=True)
        ms = jnp.where(lo, s_lo, s_hi) * (1.0 / HEAD_DIM)
        o_ref[:, c * LANES:(c + 1) * LANES] = (blk * lax.rsqrt(ms + EPS) * g128).astype(o_ref.dtype)


def _inproj_kernel(x_ref, g_ref, w_ref, b_ref, aqg_ref, akg_ref, bqg_ref, bkg_ref,
                   aq_o, ak_o, av_o, bq_o, bk_o, bv_o, z_o, gate_o, *, splits):
    h = _rms(x_ref[...], g_ref[...]).astype(BF16)

    def proj(n):
        lo, hi = splits[n], splits[n + 1]
        return _dot(h, w_ref[:, lo:hi]) + b_ref[:, lo:hi]

    _headnorm_store(proj(0), aqg_ref[...], aq_o)
    _headnorm_store(proj(1), akg_ref[...], ak_o)
    av_o[...] = proj(2).astype(av_o.dtype)
    _headnorm_store(proj(3), bqg_ref[...], bq_o)
    _headnorm_store(proj(4), bkg_ref[...], bk_o)
    bv_o[...] = proj(5).astype(bv_o.dtype)
    cu = proj(6)
    c_ch = cu.shape[1] // 2
    z_o[...] = (cu[:, :c_ch] * jax.nn.sigmoid(cu[:, c_ch:])).astype(z_o.dtype)
    gate_o[...] = jax.nn.sigmoid(proj(7)).astype(gate_o.dtype)


def _inproj(x2, g, w_in, b_in, aqg, akg, bqg, bkg, sizes, *, tm=256):
    t, d = x2.shape
    splits = tuple(int(v) for v in np.concatenate([[0], np.cumsum(sizes)]))
    a_qk, _, a_v, b_w, _, _, c2, gate_w = sizes
    widths = (a_qk, a_qk, a_v, b_w, b_w, b_w, c2 // 2, gate_w)
    dtypes = (BF16, BF16, BF16, BF16, BF16, BF16, F32, BF16)
    row = lambda i: (i, 0)
    return pl.pallas_call(
        functools.partial(_inproj_kernel, splits=splits),
        out_shape=[jax.ShapeDtypeStruct((t, w), dt) for w, dt in zip(widths, dtypes)],
        grid=(t // tm,),
        in_specs=[pl.BlockSpec((tm, d), row), _resident((1, d)), _resident(w_in.shape),
                  _resident(b_in.shape)] + [_resident((1, LANES))] * 4,
        out_specs=[pl.BlockSpec((tm, w), row) for w in widths],
        compiler_params=_params("parallel"),
        name="inproj",
    )(x2, g, w_in, b_in, aqg, akg, bqg, bkg)


def _diffattn_kernel(slopes_ref, lam_ref, gsub_ref, q_ref, k_ref, v_ref, o_ref, *,
                     tq, tk, seq, lam0):
    hd = pl.program_id(1)
    q0 = pl.program_id(2) * tq
    slope = slopes_ref[hd]
    lane = lax.broadcasted_iota(jnp.int32, (1, LANES), 1)
    lo = lane < HEAD_DIM
    q = q_ref[...]
    zero = jnp.zeros_like(q)
    q1 = jnp.where(lo, q, zero)
    q2 = jnp.where(lo, zero, q)
    rel = (lax.broadcasted_iota(jnp.int32, (tq, tk), 0)
           - lax.broadcasted_iota(jnp.int32, (tq, tk), 1)) + q0

    def step(j, carry):
        m1, l1, a1, m2, l2, a2 = carry
        k0 = pl.multiple_of(j * tk, tk)
        k = k_ref[pl.ds(k0, tk), :]
        v = v_ref[pl.ds(k0, tk), :]
        bias = jnp.abs(rel - k0).astype(F32) * slope

        def online(qm, m, l, a):
            s = _dot_nt(qm, k) - bias
            m_new = jnp.maximum(m, jnp.max(s, axis=-1, keepdims=True))
            alpha = jnp.exp(m - m_new)
            p = jnp.exp(s - m_new)
            l_new = alpha * l + jnp.sum(p, axis=-1, keepdims=True)
            a_new = alpha * a + _dot(p.astype(BF16), v)
            return m_new, l_new, a_new

        m1, l1, a1 = online(q1, m1, l1, a1)
        m2, l2, a2 = online(q2, m2, l2, a2)
        return m1, l1, a1, m2, l2, a2

    m_init = jnp.full((tq, 1), NEG_BIG, F32)
    l_init = jnp.zeros((tq, 1), F32)
    a_init = jnp.zeros((tq, LANES), F32)
    _, l1, a1, _, l2, a2 = lax.fori_loop(
        0, seq // tk, step, (m_init, l_init, a_init, m_init, l_init, a_init))

    lp = lam_ref[...]
    d1 = jnp.sum(lp[0:1] * lp[1:2], axis=-1, keepdims=True)
    d2 = jnp.sum(lp[2:3] * lp[3:4], axis=-1, keepdims=True)
    lam = jnp.exp(d1) - jnp.exp(d2) + lam0
    y = a1 / l1 - lam * (a2 / l2)
    o_ref[...] = (_rms(y, gsub_ref[...]) * (1.0 - lam0)).astype(o_ref.dtype)


def _diffattn(aq, ak, av, slopes, lam_p, gsub, *, lam0, tq=256, tk=512):
    bn, seq, _ = aq.shape
    smem = pl.BlockSpec(memory_space=pltpu.SMEM)
    return pl.pallas_call(
        functools.partial(_diffattn_kernel, tq=tq, tk=tk, seq=seq, lam0=lam0),
        out_shape=jax.ShapeDtypeStruct(av.shape, BF16),
        grid=(bn, A_HEADS, seq // tq),
        in_specs=[smem, _resident(lam_p.shape), _resident((1, LANES)),
                  pl.BlockSpec((None, tq, LANES), lambda b, h, i: (b, i, h)),
                  pl.BlockSpec((None, seq, LANES), lambda b, h, i: (b, 0, h)),
                  pl.BlockSpec((None, seq, LANES), lambda b, h, i: (b, 0, h))],
        out_specs=pl.BlockSpec((None, tq, LANES), lambda b, h, i: (b, i, h)),
        compiler_params=_params("parallel", "parallel", "parallel"),
        name="diffattn",
    )(slopes, lam_p, gsub, aq, ak, av)


def _dilated_kernel(slopes_ref, q_ref, k_ref, v_ref, o_ref, lse_ref, *, dil, u_len, tu):
    win = tu + 2 * DIL_RADIUS
    u0 = pl.program_id(2) * tu
    start = pl.multiple_of(jnp.clip(u0 - DIL_RADIUS, 0, u_len - win), DIL_RADIUS)
    du = jnp.abs((lax.broadcasted_iota(jnp.int32, (tu, win), 0) + u0)
                 - (lax.broadcasted_iota(jnp.int32, (tu, win), 1) + start))
    valid = du <= DIL_RADIUS
    dist = (du * dil).astype(F32)
    lane = lax.broadcasted_iota(jnp.int32, (1, LANES), 1)
    lo = lane < HEAD_DIM
    lse_all = jnp.zeros((tu, LANES), F32)
    for pair in range(B_HEADS // 2):
        cols = slice(pair * LANES, (pair + 1) * LANES)
        q = q_ref[:, cols]
        kw = k_ref[pl.ds(start, win), cols]
        vw = v_ref[pl.ds(start, win), cols]
        zero = jnp.zeros_like(q)
        out = None
        for half in range(2):
            hd = 2 * pair + half
            qm = jnp.where(lo, q, zero) if half == 0 else jnp.where(lo, zero, q)
            s = jnp.where(valid, _dot_nt(qm, kw) - slopes_ref[hd] * dist, NEG_BIG)
            m = jnp.max(s, axis=-1, keepdims=True)
            p = jnp.exp(s - m)
            den = jnp.sum(p, axis=-1, keepdims=True)
            o = _dot(p.astype(BF16), vw) / den
            out = o if half == 0 else jnp.where(lo, out, o)
            lse_all = jnp.where(lane == hd, m + jnp.log(den), lse_all)
        o_ref[:, cols] = out.astype(o_ref.dtype)
    lse_ref[...] = lse_all


def _dilated(bq, bk, bv, slopes, dil, *, tu=128):
    bn, seq, b_w = bq.shape
    u_len = seq // dil
    view = lambda t: t.reshape(bn, u_len, dil * b_w)
    smem = pl.BlockSpec(memory_space=pltpu.SMEM)
    o, lse = pl.pallas_call(
        functools.partial(_dilated_kernel, dil=dil, u_len=u_len, tu=tu),
        out_shape=[jax.ShapeDtypeStruct((bn, u_len, dil * b_w), BF16),
                   jax.ShapeDtypeStruct((bn, u_len, dil * LANES), F32)],
        grid=(bn, dil, u_len // tu),
        in_specs=[smem,
                  pl.BlockSpec((None, tu, b_w), lambda b, r, i: (b, i, r)),
                  pl.BlockSpec((None, u_len, b_w), lambda b, r, i: (b, 0, r)),
                  pl.BlockSpec((None, u_len, b_w), lambda b, r, i: (b, 0, r))],
        out_specs=[pl.BlockSpec((None, tu, b_w), lambda b, r, i: (b, i, r)),
                   pl.BlockSpec((None, tu, LANES), lambda b, r, i: (b, i, r))],
        compiler_params=_params("parallel", "parallel", "parallel"),
        name=f"dilated{dil}",
    )(slopes, view(bq), view(bk), view(bv))
    return o.reshape(bn * seq, b_w), lse.reshape(bn * seq, LANES)


def _conv_kernel(prev_ref, cur_ref, next_ref, w_ref, b_ref, g_ref, o_ref, pad_ref, cv_ref, *, ts, tr):
    i = pl.program_id(1)
    n = pl.num_programs(1)
    prev = prev_ref[...]
    nxt = next_ref[...]
    pad_ref[0:CONV_HALO, :] = jnp.where(i > 0, prev, jnp.zeros_like(prev))
    pad_ref[CONV_HALO:CONV_HALO + ts, :] = cur_ref[...]
    pad_ref[CONV_HALO + ts:, :] = jnp.where(i < n - 1, nxt, jnp.zeros_like(nxt))
    base = CONV_HALO - CONV_W // 2
    wrows = tr + 2 * CONV_HALO

    def rows(r, carry):
        t0 = pl.multiple_of(r * tr, tr)
        for c in range(o_ref.shape[1] // LANES):
            cols = slice(c * LANES, (c + 1) * LANES)
            win = pad_ref[pl.ds(t0, wrows), cols]
            acc = jnp.zeros((tr, LANES), F32) + b_ref[:, cols]
            for b in range(SUBLANES):
                wb = win if b == 0 else pltpu.roll(win, shift=wrows - b, axis=0)
                for a in range(2 * CONV_HALO // SUBLANES):
                    tap = SUBLANES * a + b - base
                    if 0 <= tap < CONV_W:
                        acc = acc + wb[SUBLANES * a:SUBLANES * a + tr, :] * w_ref[tap:tap + 1, cols]
            cv_ref[pl.ds(t0, tr), cols] = acc
        y = _rms(cv_ref[pl.ds(t0, tr), :], g_ref[...])
        o_ref[pl.ds(t0, tr), :] = (y * jax.nn.sigmoid(y)).astype(o_ref.dtype)
        return carry

    lax.fori_loop(0, ts // tr, rows, 0)


def _conv(z, dw_w, dw_b, norm_g, *, ts=1024, tr=64):
    bn, seq, c = z.shape
    per = ts // CONV_HALO
    last = seq // CONV_HALO - 1
    return pl.pallas_call(
        functools.partial(_conv_kernel, ts=ts, tr=tr),
        out_shape=jax.ShapeDtypeStruct((bn, seq, c), BF16),
        grid=(bn, seq // ts),
        in_specs=[pl.BlockSpec((None, CONV_HALO, c), lambda b, i: (b, jnp.maximum(i * per - 1, 0), 0)),
                  pl.BlockSpec((None, ts, c), lambda b, i: (b, i, 0)),
                  pl.BlockSpec((None, CONV_HALO, c), lambda b, i: (b, jnp.minimum((i + 1) * per, last), 0)),
                  _resident(dw_w.shape), _resident((1, c)), _resident((1, c))],
        out_specs=pl.BlockSpec((None, ts, c), lambda b, i: (b, i, 0)),
        scratch_shapes=[pltpu.VMEM((ts + 2 * CONV_HALO, c), F32), pltpu.VMEM((ts, c), F32)],
        compiler_params=_params("parallel", "parallel"),
        name="conv",
    )(z, z, z, dw_w, dw_b, norm_g)


def _merge_kernel(x_ref, ya_ref, o1_ref, o2_ref, o3_ref, l1_ref, l2_ref, l3_ref, yc_ref, gate_ref,
                  exp_ref, wa_ref, wb_ref, wc_ref, wo_ref, out_ref, *, d):
    lses = (l1_ref[...], l2_ref[...], l3_ref[...])
    top = jnp.maximum(jnp.maximum(lses[0], lses[1]), lses[2])
    es = [jnp.exp(l - top) for l in lses]
    inv = 1.0 / (es[0] + es[1] + es[2])
    mix = None
    for e, o_ref in zip(es, (o1_ref, o2_ref, o3_ref)):
        w = e * inv
        w_hi = w.astype(BF16)
        w_lo = (w - w_hi.astype(F32)).astype(BF16)
        wide = _dot(w_hi, exp_ref[...]) + _dot(w_lo, exp_ref[...])
        term = wide * o_ref[...].astype(F32)
        mix = term if mix is None else mix + term
    ya = _dot(ya_ref[...], wa_ref[...])
    yb = _dot(mix.astype(BF16), wb_ref[...])
    yc = _dot(yc_ref[...], wc_ref[...])
    merged = (gate_ref[:, 0:d].astype(F32) * ya + gate_ref[:, d:2 * d].astype(F32) * yb
              + gate_ref[:, 2 * d:3 * d].astype(F32) * yc)
    out_ref[...] = x_ref[...] + _dot(merged.astype(BF16), wo_ref[...])


def _merge(x2, ya, outs, lses, yc, gates, expand, wa, wb, wc, wo, *, tm=512):
    t, d = x2.shape
    row = lambda i: (i, 0)
    tiles = [x2, ya, *outs, *lses, yc, gates]
    weights = [expand, wa, wb, wc, wo]
    return pl.pallas_call(
        functools.partial(_merge_kernel, d=d),
        out_shape=jax.ShapeDtypeStruct((t, d), F32),
        grid=(t // tm,),
        in_specs=[pl.BlockSpec((tm, a.shape[1]), row) for a in tiles]
                 + [_resident(w.shape) for w in weights],
        out_specs=pl.BlockSpec((tm, d), row),
        compiler_params=_params("parallel"),
        name="merge",
    )(*tiles, *weights)


def _alibi_slopes(n):
    return jnp.asarray(2.0 ** (-8.0 * np.arange(1, n + 1) / n), dtype=F32)


def _pair_gain(g, scale=1.0):
    return (jnp.concatenate([g, g]) * scale).reshape(1, LANES).astype(F32)


def kernel(x, ffn1_norm, ffn1_w_up, ffn1_w_down, mix_norm, w_in, b_in, a_q_norm, a_k_norm, a_lambda, a_sub_norm, w_out_a, b_q_norm, b_k_norm, w_out_b, c_dw_w, c_dw_b, c_norm, w_out_c, w_out, ffn2_norm, ffn2_w_up, ffn2_w_down):
    bn, seq, d = x.shape
    depth = w_in.shape[0]
    a_w = w_out_a.shape[1]
    b_w = w_out_b.shape[1]
    c_ch = w_out_c.shape[1]
    sizes = (a_w, a_w, a_w, b_w, b_w, b_w, 2 * c_ch, 3 * d)
    slopes_a = _alibi_slopes(A_HEADS)
    slopes_b = _alibi_slopes(B_HEADS)
    expand = (jnp.arange(LANES)[:, None] == (jnp.arange(b_w)[None, :] // HEAD_DIM)).astype(BF16)
    row = lambda v: v.reshape(1, -1).astype(F32)

    x2 = x.reshape(bn * seq, d)
    for l in range(depth):
        x2 = _ffn(x2, row(ffn1_norm[l]), ffn1_w_up[l].astype(BF16), ffn1_w_down[l].astype(BF16))

        aq, ak, av, bq, bk, bv, z, gates = _inproj(
            x2, row(mix_norm[l]), w_in[l].astype(BF16), row(b_in[l]),
            _pair_gain(a_q_norm[l], ATTN_SCALE), _pair_gain(a_k_norm[l]),
            _pair_gain(b_q_norm[l], ATTN_SCALE), _pair_gain(b_k_norm[l]), sizes)

        lam0 = 0.8 - 0.6 * math.exp(-0.3 * l)
        to3 = lambda t: t.reshape(bn, seq, t.shape[-1])
        ya = _diffattn(to3(aq), to3(ak), to3(av), slopes_a, a_lambda[l].astype(F32),
                       row(a_sub_norm[l]), lam0=lam0).reshape(bn * seq, a_w)

        outs, lses = [], []
        for _, dil in DILATED_PATTERNS:
            o, lse = _dilated(to3(bq), to3(bk), to3(bv), slopes_b, dil)
            outs.append(o)
            lses.append(lse)

        yc = _conv(to3(z), c_dw_w[l].astype(F32), row(c_dw_b[l]), row(c_norm[l])).reshape(bn * seq, c_ch)

        x2 = _merge(x2, ya, outs, lses, yc, gates, expand,
                    w_out_a[l].astype(BF16), w_out_b[l].astype(BF16), w_out_c[l].astype(BF16),
                    w_out[l].astype(BF16))

        x2 = _ffn(x2, row(ffn2_norm[l]), ffn2_w_up[l].astype(BF16), ffn2_w_down[l].astype(BF16))
    return x2.reshape(bn, seq, d)
```

```python
import functools
import math

import jax
import jax.numpy as jnp
import numpy as np
from jax import lax
from jax.experimental import pallas as pl
from jax.experimental.pallas import tpu as pltpu

HEAD_DIM = 64
LANES = 128
SUBLANES = 8
BF16_ROWS = 16
A_HEADS = 8
B_HEADS = 12
CONV_W = 31
CONV_HALO = 16
DIL_RADIUS = 64
DIL_TU = 128
DIL_TOKENS = 512
CHUNK = 256
DEN_ROW = 16
DILATED_PATTERNS = ((128, 1), (512, 4), (2048, 16))
ATTN_SCALE = HEAD_DIM ** -0.5
LOG2E = math.log2(math.e)
EPS = 1e-6
NEG_BIG = -1e30
VMEM_LIMIT = 56 * 1024 * 1024

BF16 = jnp.bfloat16
F32 = jnp.float32


def _params(*sem):
    return pltpu.CompilerParams(dimension_semantics=sem, vmem_limit_bytes=VMEM_LIMIT)


def _resident(shape):
    nd = len(shape)
    return pl.BlockSpec(shape, lambda *_: (0,) * nd, pipeline_mode=pl.Buffered(1))


def _rms(xf, g):
    ms = jnp.mean(xf * xf, axis=-1, keepdims=True)
    return xf * lax.rsqrt(ms + EPS) * g


def _dot(a, b):
    return jnp.dot(a, b, preferred_element_type=F32)


def _dot_nt(a, b):
    return lax.dot_general(a, b, (((1,), (1,)), ((), ())), preferred_element_type=F32)


def _dot_tn(a, b):
    return lax.dot_general(a, b, (((0,), (0,)), ((), ())), preferred_element_type=F32)


def _ffn_kernel(x_ref, g_ref, wu_ref, wd_ref, o_ref, *, d_ff, n_chunks):
    x = x_ref[...]
    h = _rms(x, g_ref[...]).astype(BF16)
    ck = d_ff // n_chunks
    y = jnp.zeros_like(x)
    for c in range(n_chunks):
        a = _dot(h, wu_ref[:, c * ck:(c + 1) * ck])
        b = _dot(h, wu_ref[:, d_ff + c * ck:d_ff + (c + 1) * ck])
        act = (a * jax.nn.sigmoid(a) * b).astype(BF16)
        y = y + _dot(act, wd_ref[c * ck:(c + 1) * ck, :])
    o_ref[...] = x + 0.5 * y


def _ffn(x2, g, w_up, w_down, *, tm=512):
    t, d = x2.shape
    d_ff = w_down.shape[0]
    return pl.pallas_call(
        functools.partial(_ffn_kernel, d_ff=d_ff, n_chunks=2),
        out_shape=jax.ShapeDtypeStruct((t, d), F32),
        grid=(t // tm,),
        in_specs=[pl.BlockSpec((tm, d), lambda i: (i, 0)),
                  _resident((1, d)), _resident(w_up.shape), _resident(w_down.shape)],
        out_specs=pl.BlockSpec((tm, d), lambda i: (i, 0)),
        compiler_params=_params("parallel"),
        name="ffn",
    )(x2, g, w_up, w_down)


def _headnorm_store(y, g128, o_ref):
    lane = lax.broadcasted_iota(jnp.int32, (1, LANES), 1)
    lo = lane < HEAD_DIM
    for c in range(y.shape[1] // LANES):
        blk = y[:, c * LANES:(c + 1) * LANES]
        sq = blk * blk
        s_lo = jnp.sum(jnp.where(lo, sq, 0.0), axis=-1, keepdims=True)
        s_hi = jnp.sum(jnp.where(lo, 0.0, sq), axis=-1, keepdims=True)
        ms = jnp.where(lo, s_lo, s_hi) * (1.0 / HEAD_DIM)
        o_ref[:, c * LANES:(c + 1) * LANES] = (blk * lax.rsqrt(ms + EPS) * g128).astype(o_ref.dtype)


def _inproj_kernel(x_ref, g_ref, w_ref, b_ref, aqg_ref, akg_ref, bqg_ref, bkg_ref, perm_ref,
                   aq_o, ak_o, av_o, bq_o, bk_o, bv_o, z_o, gate_o, *regrouped, splits, dils):
    h = _rms(x_ref[...], g_ref[...]).astype(BF16)

    def proj(n):
        lo, hi = splits[n], splits[n + 1]
        return _dot(h, w_ref[:, lo:hi]) + b_ref[:, lo:hi]

    _headnorm_store(proj(0), aqg_ref[...], aq_o)
    _headnorm_store(proj(1), akg_ref[...], ak_o)
    av_o[...] = proj(2).astype(av_o.dtype)
    _headnorm_store(proj(3), bqg_ref[...], bq_o)
    _headnorm_store(proj(4), bkg_ref[...], bk_o)
    bv_o[...] = proj(5).astype(bv_o.dtype)
    cu = proj(6)
    c_ch = cu.shape[1] // 2
    z_o[...] = (cu[:, :c_ch] * jax.nn.sigmoid(cu[:, c_ch:])).astype(z_o.dtype)
    gate_o[...] = jax.nn.sigmoid(proj(7)).astype(gate_o.dtype)

    for n, dil in enumerate(dils):
        perm = perm_ref[n]
        qc_o, kg_o, vg_o = regrouped[3 * n:3 * n + 3]
        qc_o[...] = _dot(perm, bq_o[...]).astype(qc_o.dtype)
        group = CHUNK // dil
        for src, dst in ((bk_o, kg_o), (bv_o, vg_o)):
            moved = _dot(perm, src[...]).astype(dst.dtype)
            for r in range(dil):
                dst[r] = moved[r * group:(r + 1) * group, :]


def _chunk_perm(dil):
    src = np.arange(CHUNK)
    dst = (src % dil) * (CHUNK // dil) + src // dil
    perm = np.zeros((CHUNK, CHUNK), np.float32)
    perm[dst, src] = 1.0
    return perm


def _inproj(x2, g, w_in, b_in, aqg, akg, bqg, bkg, sizes, bn, dils):
    t, d = x2.shape
    tm = CHUNK
    n_chunks = t // bn // CHUNK
    splits = tuple(int(v) for v in np.concatenate([[0], np.cumsum(sizes)]))
    a_qk, _, a_v, b_w, _, _, c2, gate_w = sizes
    widths = (a_qk, a_qk, a_v, b_w, b_w, b_w, c2 // 2, gate_w)
    dtypes = (BF16, BF16, BF16, BF16, BF16, BF16, F32, BF16)
    row = lambda i: (i, 0)
    out_shape = [jax.ShapeDtypeStruct((t, w), dt) for w, dt in zip(widths, dtypes)]
    out_specs = [pl.BlockSpec((tm, w), row) for w in widths]
    for dil in dils:
        group = CHUNK // dil
        grouped = jax.ShapeDtypeStruct((bn, dil, n_chunks, group, b_w), BF16)
        grouped_spec = pl.BlockSpec((None, dil, None, group, b_w),
                                    lambda i: (i // n_chunks, 0, i % n_chunks, 0, 0))
        out_shape += [jax.ShapeDtypeStruct((t, b_w), BF16), grouped, grouped]
        out_specs += [pl.BlockSpec((tm, b_w), row), grouped_spec, grouped_spec]
    perms = jnp.asarray(np.stack([_chunk_perm(dil) for dil in dils]), BF16)
    return pl.pallas_call(
        functools.partial(_inproj_kernel, splits=splits, dils=dils),
        out_shape=out_shape,
        grid=(t // tm,),
        in_specs=[pl.BlockSpec((tm, d), row), _resident((1, d)), _resident(w_in.shape),
                  _resident(b_in.shape)] + [_resident((1, LANES))] * 4 + [_resident(perms.shape)],
        out_specs=out_specs,
        compiler_params=_params("parallel"),
        name="inproj",
    )(x2, g, w_in, b_in, aqg, akg, bqg, bkg, perms)


def _diffattn_kernel(slopes_ref, lam_ref, gsub_ref, qb_ref, kb_ref, q_ref, k_ref, v_ref, o_ref,
                     vt_ref, t_ref, s_ref, m_ref, acc_ref, *, tq, seq, lam0):
    hd = pl.program_id(1)
    qi = pl.program_id(2)
    n_tiles = seq // tq
    sig = slopes_ref[hd]

    @pl.when(qi == 0)
    def _():
        for c in range(n_tiles):
            vt_ref[c, :LANES, :] = v_ref[c * tq:(c + 1) * tq, :].astype(F32).T.astype(BF16)
            vt_ref[c, LANES:, :] = jnp.ones((BF16_ROWS, tq), BF16)
        t = -jnp.abs(lax.broadcasted_iota(jnp.int32, (tq, tq), 0)
                     - lax.broadcasted_iota(jnp.int32, (tq, tq), 1)).astype(F32) * sig
        t_ref[:, :tq] = t
        t_ref[:, tq:] = t

    lane = lax.broadcasted_iota(jnp.int32, (1, LANES), 1)
    lo = lane < HEAD_DIM
    q = q_ref[...]
    zero = jnp.zeros_like(q)
    qc = jnp.concatenate([jnp.where(lo, q, zero), jnp.where(lo, zero, q)], axis=0)
    qb = qb_ref[...]
    qc_aug = jnp.concatenate([qc, jnp.concatenate([qb, qb], axis=0)], axis=1)

    def key_tile(j):
        return k_ref[pl.ds(pl.multiple_of(j * tq, tq), tq), :]

    def tile_index(jj):
        return jnp.where(jj == 0, qi, jj - (jj <= qi).astype(jnp.int32))

    def scores(jj, slot):
        j = tile_index(jj)
        after = (j > qi).astype(jnp.int32)
        s_ref[slot] = _dot_nt(jnp.concatenate([key_tile(j), kb_ref[after]], axis=1), qc_aug)

    def absorb(jj, slot):
        j = tile_index(jj)
        far = (jnp.abs(qi - j) * tq).astype(F32) * sig
        m = m_ref[...]
        m_new = jnp.maximum(m, jnp.max(s_ref[slot], axis=0, keepdims=True) - far)
        alpha = jnp.exp2(m - m_new)
        p = jnp.exp2(s_ref[slot] - (m_new + far)).astype(BF16)
        m_ref[...] = m_new
        acc_ref[...] = alpha * acc_ref[...] + _dot(vt_ref[j], p)

    m_ref[...] = jnp.full(m_ref.shape, NEG_BIG, F32)
    acc_ref[...] = jnp.zeros(acc_ref.shape, F32)
    s_ref[0] = _dot_nt(key_tile(qi), qc) + t_ref[...]

    def pair(i, carry):
        scores(2 * i + 1, 1)
        absorb(2 * i, 0)
        scores(2 * i + 2, 0)
        absorb(2 * i + 1, 1)
        return carry

    lax.fori_loop(0, n_tiles // 2 - 1, pair, 0)
    scores(n_tiles - 1, 1)
    absorb(n_tiles - 2, 0)
    absorb(n_tiles - 1, 1)
    acc = acc_ref[...]

    lp = lam_ref[...]
    d1 = jnp.sum(lp[0:1] * lp[1:2], axis=-1, keepdims=True)
    d2 = jnp.sum(lp[2:3] * lp[3:4], axis=-1, keepdims=True)
    lam = jnp.exp(d1) - jnp.exp(d2) + lam0
    yt = acc[:LANES] / acc[LANES:LANES + 1]
    yt = yt[:, :tq] - lam * yt[:, tq:]
    ms = jnp.mean(yt * yt, axis=0, keepdims=True)
    y = (yt * lax.rsqrt(ms + EPS)).T * (gsub_ref[...] * (1.0 - lam0))
    o_ref[...] = y.astype(o_ref.dtype)


def _split3(x):
    p1 = x.astype(BF16)
    r1 = x - p1.astype(F32)
    p2 = r1.astype(BF16)
    p3 = (r1 - p2.astype(F32)).astype(BF16)
    return [p1, p2, p3]


def _alibi_lanes(slopes, tq):
    pos = (slopes[:, None] * np.arange(tq, dtype=np.float32)[None, :]).astype(np.float32)
    one = np.ones_like(pos)
    pad = np.zeros(pos.shape + (LANES - 6,), np.float32)
    stack = lambda cols: np.concatenate([np.stack(cols, axis=-1), pad], axis=-1)
    pieces = [np.asarray(p, np.float32) for p in _split3(pos)]
    q_side = stack([one, one, one] + pieces)
    k_side = stack(pieces + [-one, -one, -one])
    return jnp.asarray(q_side, BF16), jnp.asarray(np.stack([k_side, -k_side], axis=1), BF16)


def _diffattn(aq, ak, av, slopes, lam_p, gsub, *, lam0, tq=512):
    bn, seq, _ = aq.shape
    smem = pl.BlockSpec(memory_space=pltpu.SMEM)
    qb, kb = _alibi_lanes(slopes, tq)
    return pl.pallas_call(
        functools.partial(_diffattn_kernel, tq=tq, seq=seq, lam0=lam0),
        out_shape=jax.ShapeDtypeStruct(av.shape, BF16),
        grid=(bn, A_HEADS, seq // tq),
        in_specs=[smem, _resident(lam_p.shape), _resident((1, LANES)),
                  pl.BlockSpec((None, tq, LANES), lambda b, h, i: (h, 0, 0)),
                  pl.BlockSpec((None, 2, tq, LANES), lambda b, h, i: (h, 0, 0, 0)),
                  pl.BlockSpec((None, tq, LANES), lambda b, h, i: (b, i, h)),
                  pl.BlockSpec((None, seq, LANES), lambda b, h, i: (b, 0, h)),
                  pl.BlockSpec((None, seq, LANES), lambda b, h, i: (b, 0, h))],
        out_specs=pl.BlockSpec((None, tq, LANES), lambda b, h, i: (b, i, h)),
        scratch_shapes=[pltpu.VMEM((seq // tq, LANES + BF16_ROWS, tq), BF16),
                        pltpu.VMEM((tq, 2 * tq), F32),
                        pltpu.VMEM((2, tq, 2 * tq), F32),
                        pltpu.VMEM((1, 2 * tq), F32),
                        pltpu.VMEM((LANES + BF16_ROWS, 2 * tq), F32)],
        compiler_params=_params("parallel", "parallel", "arbitrary"),
        name="diffattn",
    )(jnp.asarray(slopes), lam_p, gsub, qb, kb, aq, ak, av)


def _dilated_kernel(bias_ref, inv_ref, q_ref, k_ref, v_ref, o_ref, st_ref, n_ref, rows_ref, *,
                    dil, u_len, n_sub, per_iter):
    b_w = o_ref.shape[1]
    n_pairs = b_w // LANES
    win = DIL_TU + 2 * DIL_RADIUS
    group = CHUNK // dil if dil > 1 else DIL_TU
    n_slabs = DIL_TU // group
    lane = lax.broadcasted_iota(jnp.int32, (1, LANES), 1)
    lo = lane < HEAD_DIM
    rows_ref[...] = jnp.zeros(rows_ref.shape, F32)

    def tile(t, rows_ref):
        r = t % dil
        sub = t // dil
        u0 = (pl.program_id(1) * n_sub + sub) * DIL_TU
        start = pl.multiple_of(jnp.clip(u0 - DIL_RADIUS, 0, u_len - win), DIL_RADIUS)
        variant = (u0 - start) // DIL_RADIUS
        base = pl.multiple_of(r * u_len + start, DIL_RADIUS)

        def slab(c):
            row = (sub * n_slabs + c) * (CHUNK if dil > 1 else DIL_TU) + r * group
            return pl.ds(pl.multiple_of(row, group), group)

        for pair in range(n_pairs):
            cols = slice(pair * LANES, (pair + 1) * LANES)
            q = jnp.concatenate([q_ref[slab(c), cols] for c in range(n_slabs)], axis=0)
            zero = jnp.zeros_like(q)
            qcat = jnp.concatenate([jnp.where(lo, q, zero), jnp.where(lo, zero, q)], axis=0)
            st = _dot_nt(k_ref[pl.ds(base, win), cols], qcat) + bias_ref[pair, variant]
            m = jnp.max(st, axis=0, keepdims=True)
            p = jnp.exp2(st - m)
            rows_ref[2 * pair:2 * pair + 1, :] = m[:, :DIL_TU]
            rows_ref[2 * pair + 1:2 * pair + 2, :] = m[:, DIL_TU:]
            den = jnp.sum(p, axis=0, keepdims=True)
            rows_ref[DEN_ROW + 2 * pair:DEN_ROW + 2 * pair + 1, :] = den[:, :DIL_TU]
            rows_ref[DEN_ROW + 2 * pair + 1:DEN_ROW + 2 * pair + 2, :] = den[:, DIL_TU:]
            pb = p.astype(BF16)
            vw = v_ref[pl.ds(base, win), cols]
            o0 = _dot_tn(pb[:, :DIL_TU], vw)
            o1 = _dot_tn(pb[:, DIL_TU:], vw)
            out = jnp.where(lo, o0, o1)
            for c in range(n_slabs):
                n_ref[slab(c), cols] = out[c * group:(c + 1) * group, :].astype(n_ref.dtype)
        stats = rows_ref[...].T
        pieces = [stats] if dil == 1 else _split3(stats)
        for n, piece in enumerate(pieces):
            cols = slice(b_w + n * LANES, b_w + (n + 1) * LANES)
            for c in range(n_slabs):
                n_ref[slab(c), cols] = piece[c * group:(c + 1) * group, :].astype(n_ref.dtype)

    def tiles(it, carry):
        for n in range(per_iter):
            tile(it * per_iter + n, rows_ref.at[n])
        return carry

    lax.fori_loop(0, n_sub * dil // per_iter, tiles, 0)

    if dil == 1:
        o_ref[...] = n_ref[:, :b_w].astype(o_ref.dtype)
        st_ref[...] = n_ref[:, b_w:]
    else:
        for c in range(o_ref.shape[0] // CHUNK):
            rows = slice(c * CHUNK, (c + 1) * CHUNK)
            nat = _dot(inv_ref[...], n_ref[rows, :])
            o_ref[rows, :] = nat[:, :b_w].astype(o_ref.dtype)
            st_ref[rows, :] = (nat[:, b_w:b_w + LANES] + nat[:, b_w + LANES:b_w + 2 * LANES]
                               + nat[:, b_w + 2 * LANES:])


def _dilated_bias(slopes, dil):
    win = DIL_TU + 2 * DIL_RADIUS
    r = np.arange(win)[:, None]
    c = np.arange(DIL_TU)[None, :]
    du = np.stack([np.abs(c + off - r) for off in (0, DIL_RADIUS, 2 * DIL_RADIUS)])
    alibi = -(du * dil).astype(np.float32)[None] * slopes[:, None, None, None]
    table = np.where(du[None] <= DIL_RADIUS, alibi, np.float32(NEG_BIG))
    table = table.reshape(len(slopes) // 2, 2, 3, win, DIL_TU).transpose(0, 2, 3, 1, 4)
    return jnp.asarray(table.reshape(len(slopes) // 2, 3, win, 2 * DIL_TU), F32)


def _dilated(qc, kg, vg, slopes, dil):
    bn, seq, b_w = qc.shape
    u_len = seq // dil
    n_sub = max(1, DIL_TOKENS // (DIL_TU * dil))
    tl = n_sub * DIL_TU * dil
    per_iter = 4
    extra = LANES if dil == 1 else 3 * LANES
    n_dtype = F32 if dil == 1 else BF16
    bias = _dilated_bias(slopes, dil)
    inv = jnp.asarray(_chunk_perm(dil).T, BF16)
    o, st = pl.pallas_call(
        functools.partial(_dilated_kernel, dil=dil, u_len=u_len, n_sub=n_sub, per_iter=per_iter),
        out_shape=[jax.ShapeDtypeStruct((bn, seq, b_w), BF16),
                   jax.ShapeDtypeStruct((bn, seq, LANES), F32)],
        grid=(bn, seq // tl),
        in_specs=[_resident(bias.shape), _resident(inv.shape),
                  pl.BlockSpec((None, tl, b_w), lambda b, i: (b, i, 0)),
                  pl.BlockSpec((None, seq, b_w), lambda b, i: (b, 0, 0), pipeline_mode=pl.Buffered(1)),
                  pl.BlockSpec((None, seq, b_w), lambda b, i: (b, 0, 0), pipeline_mode=pl.Buffered(1))],
        out_specs=[pl.BlockSpec((None, tl, b_w), lambda b, i: (b, i, 0)),
                   pl.BlockSpec((None, tl, LANES), lambda b, i: (b, i, 0))],
        scratch_shapes=[pltpu.VMEM((tl, b_w + extra), n_dtype),
                        pltpu.VMEM((per_iter, LANES, DIL_TU), F32)],
        compiler_params=_params("parallel", "parallel"),
        name=f"dilated{dil}",
    )(bias, inv, qc, kg, vg)
    return o.reshape(bn * seq, b_w), st.reshape(bn * seq, LANES)


def _conv_kernel(prev_ref, cur_ref, next_ref, w_ref, b_ref, g_ref, o_ref, pad_ref, cv_ref, *, ts, tr):
    i = pl.program_id(1)
    n = pl.num_programs(1)
    prev = prev_ref[...]
    nxt = next_ref[...]
    pad_ref[0:CONV_HALO, :] = jnp.where(i > 0, prev, jnp.zeros_like(prev))
    pad_ref[CONV_HALO:CONV_HALO + ts, :] = cur_ref[...]
    pad_ref[CONV_HALO + ts:, :] = jnp.where(i < n - 1, nxt, jnp.zeros_like(nxt))
    base = CONV_HALO - CONV_W // 2
    wrows = tr + 2 * CONV_HALO

    def rows(r, carry):
        t0 = pl.multiple_of(r * tr, tr)
        for c in range(o_ref.shape[1] // LANES):
            cols = slice(c * LANES, (c + 1) * LANES)
            win = pad_ref[pl.ds(t0, wrows), cols]
            acc = jnp.zeros((tr, LANES), F32) + b_ref[:, cols]
            for b in range(SUBLANES):
                wb = win if b == 0 else pltpu.roll(win, shift=wrows - b, axis=0)
                for a in range(2 * CONV_HALO // SUBLANES):
                    tap = SUBLANES * a + b - base
                    if 0 <= tap < CONV_W:
                        acc = acc + wb[SUBLANES * a:SUBLANES * a + tr, :] * w_ref[tap:tap + 1, cols]
            cv_ref[pl.ds(t0, tr), cols] = acc
        y = _rms(cv_ref[pl.ds(t0, tr), :], g_ref[...])
        o_ref[pl.ds(t0, tr), :] = (y * jax.nn.sigmoid(y)).astype(o_ref.dtype)
        return carry

    lax.fori_loop(0, ts // tr, rows, 0)


def _conv(z, dw_w, dw_b, norm_g, *, ts=1024, tr=64):
    bn, seq, c = z.shape
    per = ts // CONV_HALO
    last = seq // CONV_HALO - 1
    return pl.pallas_call(
        functools.partial(_conv_kernel, ts=ts, tr=tr),
        out_shape=jax.ShapeDtypeStruct((bn, seq, c), BF16),
        grid=(bn, seq // ts),
        in_specs=[pl.BlockSpec((None, CONV_HALO, c), lambda b, i: (b, jnp.maximum(i * per - 1, 0), 0)),
                  pl.BlockSpec((None, ts, c), lambda b, i: (b, i, 0)),
                  pl.BlockSpec((None, CONV_HALO, c), lambda b, i: (b, jnp.minimum((i + 1) * per, last), 0)),
                  _resident(dw_w.shape), _resident((1, c)), _resident((1, c))],
        out_specs=pl.BlockSpec((None, ts, c), lambda b, i: (b, i, 0)),
        scratch_shapes=[pltpu.VMEM((ts + 2 * CONV_HALO, c), F32), pltpu.VMEM((ts, c), F32)],
        compiler_params=_params("parallel", "parallel"),
        name="conv",
    )(z, z, z, dw_w, dw_b, norm_g)


def _merge_kernel(x_ref, ya_ref, o1_ref, o2_ref, o3_ref, l1_ref, l2_ref, l3_ref, yc_ref, gate_ref,
                  exp_ref, wa_ref, wb_ref, wc_ref, wo_ref, out_ref, *, d):
    stats = (l1_ref[...], l2_ref[...], l3_ref[...])
    dens = [pltpu.roll(s, LANES - DEN_ROW, axis=1) for s in stats]
    top = jnp.maximum(jnp.maximum(stats[0], stats[1]), stats[2])
    es = [jnp.exp2(s - top) for s in stats]
    total = es[0] * dens[0] + es[1] * dens[1] + es[2] * dens[2]
    heads = lax.broadcasted_iota(jnp.int32, (1, LANES), 1) < exp_ref.shape[1] // HEAD_DIM
    inv = jnp.where(heads, 1.0 / total, 0.0)
    mix = None
    for e, o_ref in zip(es, (o1_ref, o2_ref, o3_ref)):
        w = e * inv
        w_hi = w.astype(BF16)
        w_lo = (w - w_hi.astype(F32)).astype(BF16)
        wide = _dot(w_hi, exp_ref[...]) + _dot(w_lo, exp_ref[...])
        term = wide * o_ref[...].astype(F32)
        mix = term if mix is None else mix + term
    ya = _dot(ya_ref[...], wa_ref[...])
    yb = _dot(mix.astype(BF16), wb_ref[...])
    yc = _dot(yc_ref[...], wc_ref[...])
    merged = (gate_ref[:, 0:d].astype(F32) * ya + gate_ref[:, d:2 * d].astype(F32) * yb
              + gate_ref[:, 2 * d:3 * d].astype(F32) * yc)
    out_ref[...] = x_ref[...] + _dot(merged.astype(BF16), wo_ref[...])


def _merge(x2, ya, outs, lses, yc, gates, expand, wa, wb, wc, wo, *, tm=512):
    t, d = x2.shape
    row = lambda i: (i, 0)
    tiles = [x2, ya, *outs, *lses, yc, gates]
    weights = [expand, wa, wb, wc, wo]
    return pl.pallas_call(
        functools.partial(_merge_kernel, d=d),
        out_shape=jax.ShapeDtypeStruct((t, d), F32),
        grid=(t // tm,),
        in_specs=[pl.BlockSpec((tm, a.shape[1]), row) for a in tiles]
                 + [_resident(w.shape) for w in weights],
        out_specs=pl.BlockSpec((tm, d), row),
        compiler_params=_params("parallel"),
        name="merge",
    )(*tiles, *weights)


def _alibi_slopes(n):
    return (2.0 ** (-8.0 * np.arange(1, n + 1) / n) * LOG2E).astype(np.float32)


def _pair_gain(g, scale=1.0):
    return (jnp.concatenate([g, g]) * scale).reshape(1, LANES).astype(F32)


def kernel(x, ffn1_norm, ffn1_w_up, ffn1_w_down, mix_norm, w_in, b_in, a_q_norm, a_k_norm, a_lambda, a_sub_norm, w_out_a, b_q_norm, b_k_norm, w_out_b, c_dw_w, c_dw_b, c_norm, w_out_c, w_out, ffn2_norm, ffn2_w_up, ffn2_w_down):
    bn, seq, d = x.shape
    depth = w_in.shape[0]
    a_w = w_out_a.shape[1]
    b_w = w_out_b.shape[1]
    c_ch = w_out_c.shape[1]
    sizes = (a_w, a_w, a_w, b_w, b_w, b_w, 2 * c_ch, 3 * d)
    slopes_a = _alibi_slopes(A_HEADS)
    slopes_b = _alibi_slopes(B_HEADS)
    expand = (jnp.arange(LANES)[:, None] == (jnp.arange(b_w)[None, :] // HEAD_DIM)).astype(BF16)
    row = lambda v: v.reshape(1, -1).astype(F32)

    x2 = x.reshape(bn * seq, d)
    for l in range(depth):
        x2 = _ffn(x2, row(ffn1_norm[l]), ffn1_w_up[l].astype(BF16), ffn1_w_down[l].astype(BF16))

        dils = tuple(dil for _, dil in DILATED_PATTERNS if dil > 1)
        aq, ak, av, bq, bk, bv, z, gates, *regrouped = _inproj(
            x2, row(mix_norm[l]), w_in[l].astype(BF16), row(b_in[l]),
            _pair_gain(a_q_norm[l], ATTN_SCALE * LOG2E), _pair_gain(a_k_norm[l]),
            _pair_gain(b_q_norm[l], ATTN_SCALE * LOG2E), _pair_gain(b_k_norm[l]), sizes, bn, dils)

        lam0 = 0.8 - 0.6 * math.exp(-0.3 * l)
        to3 = lambda t: t.reshape(bn, seq, t.shape[-1])
        ya = _diffattn(to3(aq), to3(ak), to3(av), slopes_a, a_lambda[l].astype(F32),
                       row(a_sub_norm[l]), lam0=lam0).reshape(bn * seq, a_w)

        outs, lses = [], []
        for _, dil in DILATED_PATTERNS:
            qkv = (bq, bk, bv) if dil == 1 else regrouped[3 * dils.index(dil):3 * dils.index(dil) + 3]
            o, lse = _dilated(*(to3(t) for t in qkv), slopes_b, dil)
            outs.append(o)
            lses.append(lse)

        yc = _conv(to3(z), c_dw_w[l].astype(F32), row(c_dw_b[l]), row(c_norm[l])).reshape(bn * seq, c_ch)

        x2 = _merge(x2, ya, outs, lses, yc, gates, expand,
                    w_out_a[l].astype(BF16), w_out_b[l].astype(BF16), w_out_c[l].astype(BF16),
                    w_out[l].astype(BF16))

        x2 = _ffn(x2, row(ffn2_norm[l]), ffn2_w_up[l].astype(BF16), ffn2_w_down[l].astype(BF16))
    return x2.reshape(bn, seq, d)
```

```python
import functools
import math

import jax
import jax.numpy as jnp
import numpy as np
from jax import lax
from jax.experimental import pallas as pl
from jax.experimental.pallas import tpu as pltpu

HEAD_DIM = 64
LANES = 128
SUBLANES = 8
BF16_ROWS = 16
A_HEADS = 8
B_HEADS = 12
CONV_W = 31
CONV_HALO = 16
DIL_RADIUS = 64
DIL_TU = 128
DIL_TOKENS = 512
CHUNK = 256
DEN_ROW = 16
DILATED_PATTERNS = ((128, 1), (512, 4), (2048, 16))
ATTN_SCALE = HEAD_DIM ** -0.5
LOG2E = math.log2(math.e)
EPS = 1e-6
NEG_BIG = -1e30
VMEM_LIMIT = 56 * 1024 * 1024

BF16 = jnp.bfloat16
F32 = jnp.float32


def _params(*sem):
    return pltpu.CompilerParams(dimension_semantics=sem, vmem_limit_bytes=VMEM_LIMIT)


def _resident(shape):
    nd = len(shape)
    return pl.BlockSpec(shape, lambda *_: (0,) * nd, pipeline_mode=pl.Buffered(1))


def _rms(xf, g):
    ms = jnp.mean(xf * xf, axis=-1, keepdims=True)
    return xf * lax.rsqrt(ms + EPS) * g


def _dot(a, b):
    return jnp.dot(a, b, preferred_element_type=F32)


def _dot_nt(a, b):
    return lax.dot_general(a, b, (((1,), (1,)), ((), ())), preferred_element_type=F32)


def _dot_tn(a, b):
    return lax.dot_general(a, b, (((0,), (0,)), ((), ())), preferred_element_type=F32)


def _ffn_kernel(x_ref, g_ref, wu_ref, wd_ref, o_ref, *, d_ff, n_chunks):
    x = x_ref[...]
    h = _rms(x, g_ref[...]).astype(BF16)
    ck = d_ff // n_chunks
    y = jnp.zeros_like(x)
    for c in range(n_chunks):
        a = _dot(h, wu_ref[:, c * ck:(c + 1) * ck])
        b = _dot(h, wu_ref[:, d_ff + c * ck:d_ff + (c + 1) * ck])
        act = (a * jax.nn.sigmoid(a) * b).astype(BF16)
        y = y + _dot(act, wd_ref[c * ck:(c + 1) * ck, :])
    o_ref[...] = x + 0.5 * y


def _ffn(x2, g, w_up, w_down, *, tm=512):
    t, d = x2.shape
    d_ff = w_down.shape[0]
    return pl.pallas_call(
        functools.partial(_ffn_kernel, d_ff=d_ff, n_chunks=2),
        out_shape=jax.ShapeDtypeStruct((t, d), F32),
        grid=(t // tm,),
        in_specs=[pl.BlockSpec((tm, d), lambda i: (i, 0)),
                  _resident((1, d)), _resident(w_up.shape), _resident(w_down.shape)],
        out_specs=pl.BlockSpec((tm, d), lambda i: (i, 0)),
        compiler_params=_params("parallel"),
        name="ffn",
    )(x2, g, w_up, w_down)


def _headnorm_store(y, g128, o_ref):
    lane = lax.broadcasted_iota(jnp.int32, (1, LANES), 1)
    lo = lane < HEAD_DIM
    for c in range(y.shape[1] // LANES):
        blk = y[:, c * LANES:(c + 1) * LANES]
        sq = blk * blk
        s_lo = jnp.sum(jnp.where(lo, sq, 0.0), axis=-1, keepdims=True)
        s_hi = jnp.sum(jnp.where(lo, 0.0, sq), axis=-1, keepdims=True)
        ms = jnp.where(lo, s_lo, s_hi) * (1.0 / HEAD_DIM)
        o_ref[:, c * LANES:(c + 1) * LANES] = (blk * lax.rsqrt(ms + EPS) * g128).astype(o_ref.dtype)


def _inproj_kernel(x_ref, g_ref, w_ref, b_ref, aqg_ref, akg_ref, bqg_ref, bkg_ref, perm_ref,
                   aq_o, ak_o, av_o, bq_o, bk_o, bv_o, z_o, gate_o, *regrouped, splits, dils):
    h = _rms(x_ref[...], g_ref[...]).astype(BF16)

    def proj(n):
        lo, hi = splits[n], splits[n + 1]
        return _dot(h, w_ref[:, lo:hi]) + b_ref[:, lo:hi]

    _headnorm_store(proj(0), aqg_ref[...], aq_o)
    _headnorm_store(proj(1), akg_ref[...], ak_o)
    av_o[...] = proj(2).astype(av_o.dtype)
    _headnorm_store(proj(3), bqg_ref[...], bq_o)
    _headnorm_store(proj(4), bkg_ref[...], bk_o)
    bv_o[...] = proj(5).astype(bv_o.dtype)
    cu = proj(6)
    c_ch = cu.shape[1] // 2
    z_o[...] = (cu[:, :c_ch] * jax.nn.sigmoid(cu[:, c_ch:])).astype(z_o.dtype)
    gate_o[...] = jax.nn.sigmoid(proj(7)).astype(gate_o.dtype)

    for n, dil in enumerate(dils):
        perm = perm_ref[n]
        qc_o, kg_o, vg_o = regrouped[3 * n:3 * n + 3]
        qc_o[...] = _dot(perm, bq_o[...]).astype(qc_o.dtype)
        group = CHUNK // dil
        for src, dst in ((bk_o, kg_o), (bv_o, vg_o)):
            moved = _dot(perm, src[...]).astype(dst.dtype)
            for r in range(dil):
                dst[r] = moved[r * group:(r + 1) * group, :]


def _chunk_perm(dil):
    src = np.arange(CHUNK)
    dst = (src % dil) * (CHUNK // dil) + src // dil
    perm = np.zeros((CHUNK, CHUNK), np.float32)
    perm[dst, src] = 1.0
    return perm


def _inproj(x2, g, w_in, b_in, aqg, akg, bqg, bkg, sizes, bn, dils):
    t, d = x2.shape
    tm = CHUNK
    n_chunks = t // bn // CHUNK
    splits = tuple(int(v) for v in np.concatenate([[0], np.cumsum(sizes)]))
    a_qk, _, a_v, b_w, _, _, c2, gate_w = sizes
    widths = (a_qk, a_qk, a_v, b_w, b_w, b_w, c2 // 2, gate_w)
    dtypes = (BF16, BF16, BF16, BF16, BF16, BF16, F32, BF16)
    row = lambda i: (i, 0)
    out_shape = [jax.ShapeDtypeStruct((t, w), dt) for w, dt in zip(widths, dtypes)]
    out_specs = [pl.BlockSpec((tm, w), row) for w in widths]
    for dil in dils:
        group = CHUNK // dil
        grouped = jax.ShapeDtypeStruct((bn, dil, n_chunks, group, b_w), BF16)
        grouped_spec = pl.BlockSpec((None, dil, None, group, b_w),
                                    lambda i: (i // n_chunks, 0, i % n_chunks, 0, 0))
        out_shape += [jax.ShapeDtypeStruct((t, b_w), BF16), grouped, grouped]
        out_specs += [pl.BlockSpec((tm, b_w), row), grouped_spec, grouped_spec]
    perms = jnp.asarray(np.stack([_chunk_perm(dil) for dil in dils]), BF16)
    return pl.pallas_call(
        functools.partial(_inproj_kernel, splits=splits, dils=dils),
        out_shape=out_shape,
        grid=(t // tm,),
        in_specs=[pl.BlockSpec((tm, d), row), _resident((1, d)), _resident(w_in.shape),
                  _resident(b_in.shape)] + [_resident((1, LANES))] * 4 + [_resident(perms.shape)],
        out_specs=out_specs,
        compiler_params=_params("parallel"),
        name="inproj",
    )(x2, g, w_in, b_in, aqg, akg, bqg, bkg, perms)


def _diffattn_kernel(slopes_ref, lam_ref, gsub_ref, qb_ref, kb_ref, q_ref, k_ref, v_ref, o_ref,
                     vt_ref, t_ref, s_ref, m_ref, acc_ref, qt_ref, top_ref, *, tq, seq, lam0):
    hd = pl.program_id(1)
    qi = pl.program_id(2)
    n_tiles = seq // tq
    sig = slopes_ref[hd]

    @pl.when(qi == 0)
    def _():
        for c in range(n_tiles):
            vt_ref[c, :LANES, :] = v_ref[c * tq:(c + 1) * tq, :].astype(F32).T.astype(BF16)
            vt_ref[c, LANES:, :] = jnp.ones((BF16_ROWS, tq), BF16)
        t = -jnp.abs(lax.broadcasted_iota(jnp.int32, (tq, tq), 0)
                     - lax.broadcasted_iota(jnp.int32, (tq, tq), 1)).astype(F32) * sig
        t_ref[:, :tq] = t
        t_ref[:, tq:] = t

    lane = lax.broadcasted_iota(jnp.int32, (1, LANES), 1)
    lo = lane < HEAD_DIM
    q = q_ref[...]
    zero = jnp.zeros_like(q)
    qc = jnp.concatenate([jnp.where(lo, q, zero), jnp.where(lo, zero, q)], axis=0)
    qb = qb_ref[...]
    qt_ref[:LANES, :] = qc.astype(F32).T.astype(BF16)
    qbt = qb.astype(F32).T.astype(BF16)
    qt_ref[LANES:, :tq] = qbt
    qt_ref[LANES:, tq:] = qbt

    def key_tile(j):
        return k_ref[pl.ds(pl.multiple_of(j * tq, tq), tq), :]

    def tile_index(jj):
        return jnp.where(jj == 0, qi, jj - (jj <= qi).astype(jnp.int32))

    def put_scores(slot, st):
        s_ref[slot] = st
        top_ref[slot] = jnp.max(st, axis=0, keepdims=True)

    def scores(jj, slot):
        j = tile_index(jj)
        after = (j > qi).astype(jnp.int32)
        put_scores(slot, _dot(jnp.concatenate([key_tile(j), kb_ref[after]], axis=1), qt_ref[...]))

    def absorb(jj, slot):
        j = tile_index(jj)
        far = (jnp.abs(qi - j) * tq).astype(F32) * sig
        m = m_ref[...]
        m_new = jnp.maximum(m, top_ref[slot] - far)
        alpha = jnp.exp2(m - m_new)
        p = jnp.exp2(s_ref[slot] - (m_new + far)).astype(BF16)
        m_ref[...] = m_new
        acc_ref[...] = alpha * acc_ref[...] + _dot(vt_ref[j], p)

    m_ref[...] = jnp.full(m_ref.shape, NEG_BIG, F32)
    acc_ref[...] = jnp.zeros(acc_ref.shape, F32)
    put_scores(0, _dot(key_tile(qi), qt_ref[:LANES, :]) + t_ref[...])

    def pair(i, carry):
        scores(2 * i + 1, 1)
        absorb(2 * i, 0)
        scores(2 * i + 2, 0)
        absorb(2 * i + 1, 1)
        return carry

    lax.fori_loop(0, n_tiles // 2 - 1, pair, 0)
    scores(n_tiles - 1, 1)
    absorb(n_tiles - 2, 0)
    absorb(n_tiles - 1, 1)
    acc = acc_ref[...]

    lp = lam_ref[...]
    d1 = jnp.sum(lp[0:1] * lp[1:2], axis=-1, keepdims=True)
    d2 = jnp.sum(lp[2:3] * lp[3:4], axis=-1, keepdims=True)
    lam = jnp.exp(d1) - jnp.exp(d2) + lam0
    yt = acc[:LANES] / acc[LANES:LANES + 1]
    yt = yt[:, :tq] - lam * yt[:, tq:]
    ms = jnp.mean(yt * yt, axis=0, keepdims=True)
    y = (yt * lax.rsqrt(ms + EPS)).T * (gsub_ref[...] * (1.0 - lam0))
    o_ref[...] = y.astype(o_ref.dtype)


def _split3(x):
    p1 = x.astype(BF16)
    r1 = x - p1.astype(F32)
    p2 = r1.astype(BF16)
    p3 = (r1 - p2.astype(F32)).astype(BF16)
    return [p1, p2, p3]


def _alibi_lanes(slopes, tq):
    pos = (slopes[:, None] * np.arange(tq, dtype=np.float32)[None, :]).astype(np.float32)
    one = np.ones_like(pos)
    pad = np.zeros(pos.shape + (LANES - 6,), np.float32)
    stack = lambda cols: np.concatenate([np.stack(cols, axis=-1), pad], axis=-1)
    pieces = [np.asarray(p, np.float32) for p in _split3(pos)]
    q_side = stack([one, one, one] + pieces)
    k_side = stack(pieces + [-one, -one, -one])
    return jnp.asarray(q_side, BF16), jnp.asarray(np.stack([k_side, -k_side], axis=1), BF16)


def _diffattn(aq, ak, av, slopes, lam_p, gsub, *, lam0, tq=512):
    bn, seq, _ = aq.shape
    smem = pl.BlockSpec(memory_space=pltpu.SMEM)
    qb, kb = _alibi_lanes(slopes, tq)
    return pl.pallas_call(
        functools.partial(_diffattn_kernel, tq=tq, seq=seq, lam0=lam0),
        out_shape=jax.ShapeDtypeStruct(av.shape, BF16),
        grid=(bn, A_HEADS, seq // tq),
        in_specs=[smem, _resident(lam_p.shape), _resident((1, LANES)),
                  pl.BlockSpec((None, tq, LANES), lambda b, h, i: (h, 0, 0)),
                  pl.BlockSpec((None, 2, tq, LANES), lambda b, h, i: (h, 0, 0, 0)),
                  pl.BlockSpec((None, tq, LANES), lambda b, h, i: (b, i, h)),
                  pl.BlockSpec((None, seq, LANES), lambda b, h, i: (b, 0, h)),
                  pl.BlockSpec((None, seq, LANES), lambda b, h, i: (b, 0, h))],
        out_specs=pl.BlockSpec((None, tq, LANES), lambda b, h, i: (b, i, h)),
        scratch_shapes=[pltpu.VMEM((seq // tq, LANES + BF16_ROWS, tq), BF16),
                        pltpu.VMEM((tq, 2 * tq), F32),
                        pltpu.VMEM((2, tq, 2 * tq), F32),
                        pltpu.VMEM((1, 2 * tq), F32),
                        pltpu.VMEM((LANES + BF16_ROWS, 2 * tq), F32),
                        pltpu.VMEM((2 * LANES, 2 * tq), BF16),
                        pltpu.VMEM((2, 1, 2 * tq), F32)],
        compiler_params=_params("parallel", "parallel", "arbitrary"),
        name="diffattn",
    )(jnp.asarray(slopes), lam_p, gsub, qb, kb, aq, ak, av)


def _dilated_kernel(bias_ref, inv_ref, q_ref, k_ref, v_ref, o_ref, st_ref, n_ref, rows_ref, *,
                    dil, u_len, n_sub, per_iter):
    b_w = o_ref.shape[1]
    n_pairs = b_w // LANES
    win = DIL_TU + 2 * DIL_RADIUS
    group = CHUNK // dil if dil > 1 else DIL_TU
    n_slabs = DIL_TU // group
    lane = lax.broadcasted_iota(jnp.int32, (1, LANES), 1)
    lo = lane < HEAD_DIM
    rows_ref[...] = jnp.zeros(rows_ref.shape, F32)

    def tile(t, rows_ref):
        r = t % dil
        sub = t // dil
        u0 = (pl.program_id(1) * n_sub + sub) * DIL_TU
        start = pl.multiple_of(jnp.clip(u0 - DIL_RADIUS, 0, u_len - win), DIL_RADIUS)
        variant = (u0 - start) // DIL_RADIUS
        base = pl.multiple_of(r * u_len + start, DIL_RADIUS)

        def slab(c):
            row = (sub * n_slabs + c) * (CHUNK if dil > 1 else DIL_TU) + r * group
            return pl.ds(pl.multiple_of(row, group), group)

        for pair in range(n_pairs):
            cols = slice(pair * LANES, (pair + 1) * LANES)
            q = jnp.concatenate([q_ref[slab(c), cols] for c in range(n_slabs)], axis=0)
            zero = jnp.zeros_like(q)
            qcat = jnp.concatenate([jnp.where(lo, q, zero), jnp.where(lo, zero, q)], axis=0)
            st = _dot_nt(k_ref[pl.ds(base, win), cols], qcat) + bias_ref[pair, variant]
            m = jnp.max(st, axis=0, keepdims=True)
            p = jnp.exp2(st - m)
            rows_ref[2 * pair:2 * pair + 1, :] = m[:, :DIL_TU]
            rows_ref[2 * pair + 1:2 * pair + 2, :] = m[:, DIL_TU:]
            den = jnp.sum(p, axis=0, keepdims=True)
            rows_ref[DEN_ROW + 2 * pair:DEN_ROW + 2 * pair + 1, :] = den[:, :DIL_TU]
            rows_ref[DEN_ROW + 2 * pair + 1:DEN_ROW + 2 * pair + 2, :] = den[:, DIL_TU:]
            pb = p.astype(BF16)
            vw = v_ref[pl.ds(base, win), cols]
            o0 = _dot_tn(pb[:, :DIL_TU], vw)
            o1 = _dot_tn(pb[:, DIL_TU:], vw)
            out = jnp.where(lo, o0, o1)
            for c in range(n_slabs):
                n_ref[slab(c), cols] = out[c * group:(c + 1) * group, :].astype(n_ref.dtype)
        stats = rows_ref[...].T
        pieces = [stats] if dil == 1 else _split3(stats)
        for n, piece in enumerate(pieces):
            cols = slice(b_w + n * LANES, b_w + (n + 1) * LANES)
            for c in range(n_slabs):
                n_ref[slab(c), cols] = piece[c * group:(c + 1) * group, :].astype(n_ref.dtype)

    def tiles(it, carry):
        for n in range(per_iter):
            tile(it * per_iter + n, rows_ref.at[n])
        return carry

    lax.fori_loop(0, n_sub * dil // per_iter, tiles, 0)

    if dil == 1:
        o_ref[...] = n_ref[:, :b_w].astype(o_ref.dtype)
        st_ref[...] = n_ref[:, b_w:]
    else:
        for c in range(o_ref.shape[0] // CHUNK):
            rows = slice(c * CHUNK, (c + 1) * CHUNK)
            nat = _dot(inv_ref[...], n_ref[rows, :])
            o_ref[rows, :] = nat[:, :b_w].astype(o_ref.dtype)
            st_ref[rows, :] = (nat[:, b_w:b_w + LANES] + nat[:, b_w + LANES:b_w + 2 * LANES]
                               + nat[:, b_w + 2 * LANES:])


def _dilated_bias(slopes, dil):
    win = DIL_TU + 2 * DIL_RADIUS
    r = np.arange(win)[:, None]
    c = np.arange(DIL_TU)[None, :]
    du = np.stack([np.abs(c + off - r) for off in (0, DIL_RADIUS, 2 * DIL_RADIUS)])
    alibi = -(du * dil).astype(np.float32)[None] * slopes[:, None, None, None]
    table = np.where(du[None] <= DIL_RADIUS, alibi, np.float32(NEG_BIG))
    table = table.reshape(len(slopes) // 2, 2, 3, win, DIL_TU).transpose(0, 2, 3, 1, 4)
    return jnp.asarray(table.reshape(len(slopes) // 2, 3, win, 2 * DIL_TU), F32)


def _dilated(qc, kg, vg, slopes, dil):
    bn, seq, b_w = qc.shape
    u_len = seq // dil
    n_sub = max(1, DIL_TOKENS // (DIL_TU * dil))
    tl = n_sub * DIL_TU * dil
    per_iter = 4
    extra = LANES if dil == 1 else 3 * LANES
    n_dtype = F32 if dil == 1 else BF16
    bias = _dilated_bias(slopes, dil)
    inv = jnp.asarray(_chunk_perm(dil).T, BF16)
    o, st = pl.pallas_call(
        functools.partial(_dilated_kernel, dil=dil, u_len=u_len, n_sub=n_sub, per_iter=per_iter),
        out_shape=[jax.ShapeDtypeStruct((bn, seq, b_w), BF16),
                   jax.ShapeDtypeStruct((bn, seq, LANES), F32)],
        grid=(bn, seq // tl),
        in_specs=[_resident(bias.shape), _resident(inv.shape),
                  pl.BlockSpec((None, tl, b_w), lambda b, i: (b, i, 0)),
                  pl.BlockSpec((None, seq, b_w), lambda b, i: (b, 0, 0), pipeline_mode=pl.Buffered(1)),
                  pl.BlockSpec((None, seq, b_w), lambda b, i: (b, 0, 0), pipeline_mode=pl.Buffered(1))],
        out_specs=[pl.BlockSpec((None, tl, b_w), lambda b, i: (b, i, 0)),
                   pl.BlockSpec((None, tl, LANES), lambda b, i: (b, i, 0))],
        scratch_shapes=[pltpu.VMEM((tl, b_w + extra), n_dtype),
                        pltpu.VMEM((per_iter, LANES, DIL_TU), F32)],
        compiler_params=_params("parallel", "parallel"),
        name=f"dilated{dil}",
    )(bias, inv, qc, kg, vg)
    return o.reshape(bn * seq, b_w), st.reshape(bn * seq, LANES)


def _conv_kernel(prev_ref, cur_ref, next_ref, w_ref, b_ref, g_ref, o_ref, pad_ref, cv_ref, *, ts, tr):
    i = pl.program_id(1)
    n = pl.num_programs(1)
    prev = prev_ref[...]
    nxt = next_ref[...]
    pad_ref[0:CONV_HALO, :] = jnp.where(i > 0, prev, jnp.zeros_like(prev))
    pad_ref[CONV_HALO:CONV_HALO + ts, :] = cur_ref[...]
    pad_ref[CONV_HALO + ts:, :] = jnp.where(i < n - 1, nxt, jnp.zeros_like(nxt))
    base = CONV_HALO - CONV_W // 2
    wrows = tr + 2 * CONV_HALO

    def rows(r, carry):
        t0 = pl.multiple_of(r * tr, tr)
        for c in range(o_ref.shape[1] // LANES):
            cols = slice(c * LANES, (c + 1) * LANES)
            win = pad_ref[pl.ds(t0, wrows), cols]
            acc = jnp.zeros((tr, LANES), F32) + b_ref[:, cols]
            for b in range(SUBLANES):
                wb = win if b == 0 else pltpu.roll(win, shift=wrows - b, axis=0)
                for a in range(2 * CONV_HALO // SUBLANES):
                    tap = SUBLANES * a + b - base
                    if 0 <= tap < CONV_W:
                        acc = acc + wb[SUBLANES * a:SUBLANES * a + tr, :] * w_ref[tap:tap + 1, cols]
            cv_ref[pl.ds(t0, tr), cols] = acc
        y = _rms(cv_ref[pl.ds(t0, tr), :], g_ref[...])
        o_ref[pl.ds(t0, tr), :] = (y * jax.nn.sigmoid(y)).astype(o_ref.dtype)
        return carry

    lax.fori_loop(0, ts // tr, rows, 0)


def _conv(z, dw_w, dw_b, norm_g, *, ts=1024, tr=64):
    bn, seq, c = z.shape
    per = ts // CONV_HALO
    last = seq // CONV_HALO - 1
    return pl.pallas_call(
        functools.partial(_conv_kernel, ts=ts, tr=tr),
        out_shape=jax.ShapeDtypeStruct((bn, seq, c), BF16),
        grid=(bn, seq // ts),
        in_specs=[pl.BlockSpec((None, CONV_HALO, c), lambda b, i: (b, jnp.maximum(i * per - 1, 0), 0)),
                  pl.BlockSpec((None, ts, c), lambda b, i: (b, i, 0)),
                  pl.BlockSpec((None, CONV_HALO, c), lambda b, i: (b, jnp.minimum((i + 1) * per, last), 0)),
                  _resident(dw_w.shape), _resident((1, c)), _resident((1, c))],
        out_specs=pl.BlockSpec((None, ts, c), lambda b, i: (b, i, 0)),
        scratch_shapes=[pltpu.VMEM((ts + 2 * CONV_HALO, c), F32), pltpu.VMEM((ts, c), F32)],
        compiler_params=_params("parallel", "parallel"),
        name="conv",
    )(z, z, z, dw_w, dw_b, norm_g)


def _merge_kernel(x_ref, ya_ref, o1_ref, o2_ref, o3_ref, l1_ref, l2_ref, l3_ref, yc_ref, gate_ref,
                  exp_ref, wa_ref, wb_ref, wc_ref, wo_ref, out_ref, *, d):
    stats = (l1_ref[...], l2_ref[...], l3_ref[...])
    dens = [pltpu.roll(s, LANES - DEN_ROW, axis=1) for s in stats]
    top = jnp.maximum(jnp.maximum(stats[0], stats[1]), stats[2])
    es = [jnp.exp2(s - top) for s in stats]
    total = es[0] * dens[0] + es[1] * dens[1] + es[2] * dens[2]
    heads = lax.broadcasted_iota(jnp.int32, (1, LANES), 1) < exp_ref.shape[1] // HEAD_DIM
    inv = jnp.where(heads, 1.0 / total, 0.0)
    mix = None
    for e, o_ref in zip(es, (o1_ref, o2_ref, o3_ref)):
        w = e * inv
        w_hi = w.astype(BF16)
        w_lo = (w - w_hi.astype(F32)).astype(BF16)
        wide = _dot(w_hi, exp_ref[...]) + _dot(w_lo, exp_ref[...])
        term = wide * o_ref[...].astype(F32)
        mix = term if mix is None else mix + term
    ya = _dot(ya_ref[...], wa_ref[...])
    yb = _dot(mix.astype(BF16), wb_ref[...])
    yc = _dot(yc_ref[...], wc_ref[...])
    merged = (gate_ref[:, 0:d].astype(F32) * ya + gate_ref[:, d:2 * d].astype(F32) * yb
              + gate_ref[:, 2 * d:3 * d].astype(F32) * yc)
    out_ref[...] = x_ref[...] + _dot(merged.astype(BF16), wo_ref[...])


def _merge(x2, ya, outs, lses, yc, gates, expand, wa, wb, wc, wo, *, tm=512):
    t, d = x2.shape
    row = lambda i: (i, 0)
    tiles = [x2, ya, *outs, *lses, yc, gates]
    weights = [expand, wa, wb, wc, wo]
    return pl.pallas_call(
        functools.partial(_merge_kernel, d=d),
        out_shape=jax.ShapeDtypeStruct((t, d), F32),
        grid=(t // tm,),
        in_specs=[pl.BlockSpec((tm, a.shape[1]), row) for a in tiles]
                 + [_resident(w.shape) for w in weights],
        out_specs=pl.BlockSpec((tm, d), row),
        compiler_params=_params("parallel"),
        name="merge",
    )(*tiles, *weights)


def _alibi_slopes(n):
    return (2.0 ** (-8.0 * np.arange(1, n + 1) / n) * LOG2E).astype(np.float32)


def _pair_gain(g, scale=1.0):
    return (jnp.concatenate([g, g]) * scale).reshape(1, LANES).astype(F32)


def kernel(x, ffn1_norm, ffn1_w_up, ffn1_w_down, mix_norm, w_in, b_in, a_q_norm, a_k_norm, a_lambda, a_sub_norm, w_out_a, b_q_norm, b_k_norm, w_out_b, c_dw_w, c_dw_b, c_norm, w_out_c, w_out, ffn2_norm, ffn2_w_up, ffn2_w_down):
    bn, seq, d = x.shape
    depth = w_in.shape[0]
    a_w = w_out_a.shape[1]
    b_w = w_out_b.shape[1]
    c_ch = w_out_c.shape[1]
    sizes = (a_w, a_w, a_w, b_w, b_w, b_w, 2 * c_ch, 3 * d)
    slopes_a = _alibi_slopes(A_HEADS)
    slopes_b = _alibi_slopes(B_HEADS)
    expand = (jnp.arange(LANES)[:, None] == (jnp.arange(b_w)[None, :] // HEAD_DIM)).astype(BF16)
    row = lambda v: v.reshape(1, -1).astype(F32)

    x2 = x.reshape(bn * seq, d)
    for l in range(depth):
        x2 = _ffn(x2, row(ffn1_norm[l]), ffn1_w_up[l].astype(BF16), ffn1_w_down[l].astype(BF16))

        dils = tuple(dil for _, dil in DILATED_PATTERNS if dil > 1)
        aq, ak, av, bq, bk, bv, z, gates, *regrouped = _inproj(
            x2, row(mix_norm[l]), w_in[l].astype(BF16), row(b_in[l]),
            _pair_gain(a_q_norm[l], ATTN_SCALE * LOG2E), _pair_gain(a_k_norm[l]),
            _pair_gain(b_q_norm[l], ATTN_SCALE * LOG2E), _pair_gain(b_k_norm[l]), sizes, bn, dils)

        lam0 = 0.8 - 0.6 * math.exp(-0.3 * l)
        to3 = lambda t: t.reshape(bn, seq, t.shape[-1])
        ya = _diffattn(to3(aq), to3(ak), to3(av), slopes_a, a_lambda[l].astype(F32),
                       row(a_sub_norm[l]), lam0=lam0).reshape(bn * seq, a_w)

        outs, lses = [], []
        for _, dil in DILATED_PATTERNS:
            qkv = (bq, bk, bv) if dil == 1 else regrouped[3 * dils.index(dil):3 * dils.index(dil) + 3]
            o, lse = _dilated(*(to3(t) for t in qkv), slopes_b, dil)
            outs.append(o)
            lses.append(lse)

        yc = _conv(to3(z), c_dw_w[l].astype(F32), row(c_dw_b[l]), row(c_norm[l])).reshape(bn * seq, c_ch)

        x2 = _merge(x2, ya, outs, lses, yc, gates, expand,
                    w_out_a[l].astype(BF16), w_out_b[l].astype(BF16), w_out_c[l].astype(BF16),
                    w_out[l].astype(BF16))

        x2 = _ffn(x2, row(ffn2_norm[l]), ffn2_w_up[l].astype(BF16), ffn2_w_down[l].astype(BF16))
    return x2.reshape(bn, seq, d)
```

```python
import functools
import math

import jax
import jax.numpy as jnp
import numpy as np
from jax import lax
from jax.experimental import pallas as pl
from jax.experimental.pallas import tpu as pltpu

HEAD_DIM = 64
LANES = 128
SUBLANES = 8
BF16_ROWS = 16
A_HEADS = 8
B_HEADS = 12
CONV_W = 31
CONV_HALO = 16
DIL_RADIUS = 64
DIL_TU = 128
DIL_TOKENS = 1024
CHUNK = 256
DEN_ROW = 16
DILATED_PATTERNS = ((128, 1), (512, 4), (2048, 16))
ATTN_SCALE = HEAD_DIM ** -0.5
LOG2E = math.log2(math.e)
EPS = 1e-6
NEG_BIG = -1e30
VMEM_LIMIT = 56 * 1024 * 1024

BF16 = jnp.bfloat16
F32 = jnp.float32


def _params(*sem):
    return pltpu.CompilerParams(dimension_semantics=sem, vmem_limit_bytes=VMEM_LIMIT)


def _resident(shape):
    nd = len(shape)
    return pl.BlockSpec(shape, lambda *_: (0,) * nd, pipeline_mode=pl.Buffered(1))


def _rms(xf, g):
    ms = jnp.mean(xf * xf, axis=-1, keepdims=True)
    return xf * lax.rsqrt(ms + EPS) * g


def _dot(a, b):
    return jnp.dot(a, b, preferred_element_type=F32)


def _dot_nt(a, b):
    return lax.dot_general(a, b, (((1,), (1,)), ((), ())), preferred_element_type=F32)


def _dot_tn(a, b):
    return lax.dot_general(a, b, (((0,), (0,)), ((), ())), preferred_element_type=F32)


def _ffn_kernel(x_ref, g_ref, wu_ref, wd_ref, o_ref, *, d_ff, n_chunks):
    x = x_ref[...]
    h = _rms(x, g_ref[...]).astype(BF16)
    ck = d_ff // n_chunks
    y = jnp.zeros_like(x)
    for c in range(n_chunks):
        a = _dot(h, wu_ref[:, c * ck:(c + 1) * ck])
        b = _dot(h, wu_ref[:, d_ff + c * ck:d_ff + (c + 1) * ck])
        act = (a * jax.nn.sigmoid(a) * b).astype(BF16)
        y = y + _dot(act, wd_ref[c * ck:(c + 1) * ck, :])
    o_ref[...] = x + 0.5 * y


def _ffn(x2, g, w_up, w_down, *, tm=512):
    t, d = x2.shape
    d_ff = w_down.shape[0]
    return pl.pallas_call(
        functools.partial(_ffn_kernel, d_ff=d_ff, n_chunks=2),
        out_shape=jax.ShapeDtypeStruct((t, d), F32),
        grid=(t // tm,),
        in_specs=[pl.BlockSpec((tm, d), lambda i: (i, 0)),
                  _resident((1, d)), _resident(w_up.shape), _resident(w_down.shape)],
        out_specs=pl.BlockSpec((tm, d), lambda i: (i, 0)),
        compiler_params=_params("parallel"),
        name="ffn",
    )(x2, g, w_up, w_down)


def _headnorm_store(y, g128, o_ref, transposed=False):
    lane = lax.broadcasted_iota(jnp.int32, (1, LANES), 1)
    lo = lane < HEAD_DIM
    for c in range(y.shape[1] // LANES):
        blk = y[:, c * LANES:(c + 1) * LANES]
        sq = blk * blk
        s_lo = jnp.sum(jnp.where(lo, sq, 0.0), axis=-1, keepdims=True)
        s_hi = jnp.sum(jnp.where(lo, 0.0, sq), axis=-1, keepdims=True)
        ms = jnp.where(lo, s_lo, s_hi) * (1.0 / HEAD_DIM)
        normed = blk * lax.rsqrt(ms + EPS) * g128
        if transposed:
            o_ref[c] = normed.T.astype(o_ref.dtype)
        else:
            o_ref[:, c * LANES:(c + 1) * LANES] = normed.astype(o_ref.dtype)


def _inproj_kernel(x_ref, g_ref, w_ref, b_ref, aqg_ref, akg_ref, bqg_ref, bkg_ref, perm_ref,
                   aq_o, ak_o, av_o, bq_o, bk_o, bv_o, z_o, gate_o, *regrouped, splits, dils):
    h = _rms(x_ref[...], g_ref[...]).astype(BF16)

    def proj(n):
        lo, hi = splits[n], splits[n + 1]
        return _dot(h, w_ref[:, lo:hi]) + b_ref[:, lo:hi]

    _headnorm_store(proj(0), aqg_ref[...], aq_o, transposed=True)
    _headnorm_store(proj(1), akg_ref[...], ak_o)
    av_o[...] = proj(2).astype(av_o.dtype)
    _headnorm_store(proj(3), bqg_ref[...], bq_o)
    _headnorm_store(proj(4), bkg_ref[...], bk_o)
    bv_o[...] = proj(5).astype(bv_o.dtype)
    cu = proj(6)
    c_ch = cu.shape[1] // 2
    z_o[...] = (cu[:, :c_ch] * jax.nn.sigmoid(cu[:, c_ch:])).astype(z_o.dtype)
    gate_o[...] = jax.nn.sigmoid(proj(7)).astype(gate_o.dtype)

    for n, dil in enumerate(dils):
        perm = perm_ref[n]
        qc_o, kg_o, vg_o = regrouped[3 * n:3 * n + 3]
        qc_o[...] = _dot(perm, bq_o[...]).astype(qc_o.dtype)
        group = CHUNK // dil
        for src, dst in ((bk_o, kg_o), (bv_o, vg_o)):
            moved = _dot(perm, src[...]).astype(dst.dtype)
            for r in range(dil):
                dst[r] = moved[r * group:(r + 1) * group, :]


def _chunk_perm(dil):
    src = np.arange(CHUNK)
    dst = (src % dil) * (CHUNK // dil) + src // dil
    perm = np.zeros((CHUNK, CHUNK), np.float32)
    perm[dst, src] = 1.0
    return perm


def _inproj(x2, g, w_in, b_in, aqg, akg, bqg, bkg, sizes, bn, dils):
    t, d = x2.shape
    tm = CHUNK
    n_chunks = t // bn // CHUNK
    splits = tuple(int(v) for v in np.concatenate([[0], np.cumsum(sizes)]))
    a_qk, _, a_v, b_w, _, _, c2, gate_w = sizes
    widths = (a_qk, a_qk, a_v, b_w, b_w, b_w, c2 // 2, gate_w)
    dtypes = (BF16, BF16, BF16, BF16, BF16, BF16, F32, BF16)
    row = lambda i: (i, 0)
    out_shape = [jax.ShapeDtypeStruct((t, w), dt) for w, dt in zip(widths, dtypes)]
    out_specs = [pl.BlockSpec((tm, w), row) for w in widths]
    out_shape[0] = jax.ShapeDtypeStruct((bn, a_qk // LANES, LANES, t // bn), BF16)
    out_specs[0] = pl.BlockSpec((None, a_qk // LANES, LANES, tm),
                                lambda i: (i // n_chunks, 0, 0, i % n_chunks))
    for dil in dils:
        group = CHUNK // dil
        grouped = jax.ShapeDtypeStruct((bn, dil, n_chunks, group, b_w), BF16)
        grouped_spec = pl.BlockSpec((None, dil, None, group, b_w),
                                    lambda i: (i // n_chunks, 0, i % n_chunks, 0, 0))
        out_shape += [jax.ShapeDtypeStruct((t, b_w), BF16), grouped, grouped]
        out_specs += [pl.BlockSpec((tm, b_w), row), grouped_spec, grouped_spec]
    perms = jnp.asarray(np.stack([_chunk_perm(dil) for dil in dils]), BF16)
    return pl.pallas_call(
        functools.partial(_inproj_kernel, splits=splits, dils=dils),
        out_shape=out_shape,
        grid=(t // tm,),
        in_specs=[pl.BlockSpec((tm, d), row), _resident((1, d)), _resident(w_in.shape),
                  _resident(b_in.shape)] + [_resident((1, LANES))] * 4 + [_resident(perms.shape)],
        out_specs=out_specs,
        compiler_params=_params("parallel"),
        name="inproj",
    )(x2, g, w_in, b_in, aqg, akg, bqg, bkg, perms)


def _diffattn_kernel(slopes_ref, lam_ref, gsub_ref, qb_ref, kb_ref, q_ref, k_ref, v_ref, o_ref,
                     vt_ref, t_ref, s_ref, m_ref, acc_ref, qt_ref, top_ref, *, tq, seq, lam0):
    hd = pl.program_id(1)
    qi = pl.program_id(2)
    n_tiles = seq // tq
    sig = slopes_ref[hd]

    @pl.when(qi == 0)
    def _():
        for c in range(n_tiles):
            vt_ref[c, :LANES, :] = v_ref[c * tq:(c + 1) * tq, :].astype(F32).T.astype(BF16)
            vt_ref[c, LANES:, :] = jnp.ones((BF16_ROWS, tq), BF16)
        t = -jnp.abs(lax.broadcasted_iota(jnp.int32, (tq, tq), 0)
                     - lax.broadcasted_iota(jnp.int32, (tq, tq), 1)).astype(F32) * sig
        t_ref[:, :tq] = t
        t_ref[:, tq:] = t

    zero = jnp.zeros((HEAD_DIM, tq), BF16)
    qt_ref[:HEAD_DIM, :tq] = q_ref[:HEAD_DIM, :]
    qt_ref[HEAD_DIM:LANES, :tq] = zero
    qt_ref[:HEAD_DIM, tq:] = zero
    qt_ref[HEAD_DIM:LANES, tq:] = q_ref[HEAD_DIM:, :]
    qt_ref[LANES:, :tq] = qb_ref[...]
    qt_ref[LANES:, tq:] = qb_ref[...]

    def key_tile(j):
        return k_ref[pl.ds(pl.multiple_of(j * tq, tq), tq), :]

    def tile_index(jj):
        return jnp.where(jj == 0, qi, jj - (jj <= qi).astype(jnp.int32))

    def put_scores(slot, st):
        s_ref[slot] = st
        top_ref[slot] = jnp.max(st, axis=0, keepdims=True)

    def scores(jj, slot):
        j = tile_index(jj)
        after = (j > qi).astype(jnp.int32)
        put_scores(slot, _dot(jnp.concatenate([key_tile(j), kb_ref[after]], axis=1), qt_ref[...]))

    def absorb(jj, slot):
        j = tile_index(jj)
        far = (jnp.abs(qi - j) * tq).astype(F32) * sig
        m = m_ref[...]
        m_new = jnp.maximum(m, top_ref[slot] - far)
        alpha = jnp.exp2(m - m_new)
        p = jnp.exp2(s_ref[slot] - (m_new + far)).astype(BF16)
        m_ref[...] = m_new
        acc_ref[...] = alpha * acc_ref[...] + _dot(vt_ref[j], p)

    m_ref[...] = jnp.full(m_ref.shape, NEG_BIG, F32)
    acc_ref[...] = jnp.zeros(acc_ref.shape, F32)
    put_scores(0, _dot(key_tile(qi), qt_ref[:LANES, :]) + t_ref[...])

    scores(1, 1)
    absorb(0, 0)

    def triple(i, carry):
        for n in range(1, 4):
            scores(3 * i + n + 1, (n + 1) % 3)
            absorb(3 * i + n, n % 3)
        return carry

    lax.fori_loop(0, (n_tiles - 2) // 3, triple, 0)
    absorb(n_tiles - 1, (n_tiles - 1) % 3)
    acc = acc_ref[...]

    lp = lam_ref[...]
    d1 = jnp.sum(lp[0:1] * lp[1:2], axis=-1, keepdims=True)
    d2 = jnp.sum(lp[2:3] * lp[3:4], axis=-1, keepdims=True)
    lam = jnp.exp(d1) - jnp.exp(d2) + lam0
    yt = acc[:LANES] / acc[LANES:LANES + 1]
    yt = yt[:, :tq] - lam * yt[:, tq:]
    ms = jnp.mean(yt * yt, axis=0, keepdims=True)
    o_ref[...] = (yt * lax.rsqrt(ms + EPS) * (gsub_ref[...] * (1.0 - lam0))).astype(o_ref.dtype)


def _split3(x):
    p1 = x.astype(BF16)
    r1 = x - p1.astype(F32)
    p2 = r1.astype(BF16)
    p3 = (r1 - p2.astype(F32)).astype(BF16)
    return [p1, p2, p3]


def _alibi_lanes(slopes, tq):
    pos = (slopes[:, None] * np.arange(tq, dtype=np.float32)[None, :]).astype(np.float32)
    one = np.ones_like(pos)
    pad = np.zeros(pos.shape + (LANES - 6,), np.float32)
    stack = lambda cols: np.concatenate([np.stack(cols, axis=-1), pad], axis=-1)
    pieces = [np.asarray(p, np.float32) for p in _split3(pos)]
    q_side = stack([one, one, one] + pieces)
    k_side = stack(pieces + [-one, -one, -one])
    return (jnp.asarray(q_side.transpose(0, 2, 1), BF16),
            jnp.asarray(np.stack([k_side, -k_side], axis=1), BF16))


def _diffattn(aqt, ak, av, slopes, lam_p, gsub, *, lam0, tq=512):
    bn, seq, _ = ak.shape
    smem = pl.BlockSpec(memory_space=pltpu.SMEM)
    qb, kb = _alibi_lanes(slopes, tq)
    return pl.pallas_call(
        functools.partial(_diffattn_kernel, tq=tq, seq=seq, lam0=lam0),
        out_shape=jax.ShapeDtypeStruct(aqt.shape, BF16),
        grid=(bn, A_HEADS, seq // tq),
        in_specs=[smem, _resident(lam_p.shape), _resident((LANES, 1)),
                  pl.BlockSpec((None, LANES, tq), lambda b, h, i: (h, 0, 0)),
                  pl.BlockSpec((None, 2, tq, LANES), lambda b, h, i: (h, 0, 0, 0)),
                  pl.BlockSpec((None, None, LANES, tq), lambda b, h, i: (b, h, 0, i)),
                  pl.BlockSpec((None, seq, LANES), lambda b, h, i: (b, 0, h)),
                  pl.BlockSpec((None, seq, LANES), lambda b, h, i: (b, 0, h))],
        out_specs=pl.BlockSpec((None, None, LANES, tq), lambda b, h, i: (b, h, 0, i)),
        scratch_shapes=[pltpu.VMEM((seq // tq, LANES + BF16_ROWS, tq), BF16),
                        pltpu.VMEM((tq, 2 * tq), F32),
                        pltpu.VMEM((3, tq, 2 * tq), F32),
                        pltpu.VMEM((1, 2 * tq), F32),
                        pltpu.VMEM((LANES + BF16_ROWS, 2 * tq), F32),
                        pltpu.VMEM((2 * LANES, 2 * tq), BF16),
                        pltpu.VMEM((3, 1, 2 * tq), F32)],
        compiler_params=_params("parallel", "parallel", "arbitrary"),
        name="diffattn",
    )(jnp.asarray(slopes), lam_p, gsub, qb, kb, aqt, ak, av)


def _dilated_kernel(bias_ref, inv_ref, q_ref, k_ref, v_ref, o_ref, st_ref, n_ref, rows_ref, *,
                    dil, u_len, n_sub, per_iter):
    b_w = o_ref.shape[1]
    n_pairs = b_w // LANES
    win = DIL_TU + 2 * DIL_RADIUS
    group = CHUNK // dil if dil > 1 else DIL_TU
    n_slabs = DIL_TU // group
    lane = lax.broadcasted_iota(jnp.int32, (1, LANES), 1)
    lo = lane < HEAD_DIM
    rows_ref[...] = jnp.zeros(rows_ref.shape, F32)

    def tile(t, rows_ref):
        r = t % dil
        sub = t // dil
        u0 = (pl.program_id(1) * n_sub + sub) * DIL_TU
        start = pl.multiple_of(jnp.clip(u0 - DIL_RADIUS, 0, u_len - win), DIL_RADIUS)
        variant = (u0 - start) // DIL_RADIUS
        base = pl.multiple_of(r * u_len + start, DIL_RADIUS)

        def slab(c):
            row = (sub * n_slabs + c) * (CHUNK if dil > 1 else DIL_TU) + r * group
            return pl.ds(pl.multiple_of(row, group), group)

        for pair in range(n_pairs):
            cols = slice(pair * LANES, (pair + 1) * LANES)
            q = jnp.concatenate([q_ref[slab(c), cols] for c in range(n_slabs)], axis=0)
            zero = jnp.zeros_like(q)
            qcat = jnp.concatenate([jnp.where(lo, q, zero), jnp.where(lo, zero, q)], axis=0)
            st = _dot_nt(k_ref[pl.ds(base, win), cols], qcat) + bias_ref[pair, variant]
            m = jnp.max(st, axis=0, keepdims=True)
            p = jnp.exp2(st - m)
            rows_ref[2 * pair:2 * pair + 1, :] = m[:, :DIL_TU]
            rows_ref[2 * pair + 1:2 * pair + 2, :] = m[:, DIL_TU:]
            den = jnp.sum(p, axis=0, keepdims=True)
            rows_ref[DEN_ROW + 2 * pair:DEN_ROW + 2 * pair + 1, :] = den[:, :DIL_TU]
            rows_ref[DEN_ROW + 2 * pair + 1:DEN_ROW + 2 * pair + 2, :] = den[:, DIL_TU:]
            pb = p.astype(BF16)
            vw = v_ref[pl.ds(base, win), cols]
            o0 = _dot_tn(pb[:, :DIL_TU], vw)
            o1 = _dot_tn(pb[:, DIL_TU:], vw)
            out = jnp.where(lo, o0, o1)
            for c in range(n_slabs):
                n_ref[slab(c), cols] = out[c * group:(c + 1) * group, :].astype(n_ref.dtype)
        stats = rows_ref[...].T
        pieces = [stats] if dil == 1 else _split3(stats)
        for n, piece in enumerate(pieces):
            cols = slice(b_w + n * LANES, b_w + (n + 1) * LANES)
            for c in range(n_slabs):
                n_ref[slab(c), cols] = piece[c * group:(c + 1) * group, :].astype(n_ref.dtype)

    def tiles(it, carry):
        for n in range(per_iter):
            tile(it * per_iter + n, rows_ref.at[n])
        return carry

    lax.fori_loop(0, n_sub * dil // per_iter, tiles, 0)

    if dil == 1:
        o_ref[...] = n_ref[:, :b_w].astype(o_ref.dtype)
        st_ref[...] = n_ref[:, b_w:]
    else:
        for c in range(o_ref.shape[0] // CHUNK):
            rows = slice(c * CHUNK, (c + 1) * CHUNK)
            nat = _dot(inv_ref[...], n_ref[rows, :])
            o_ref[rows, :] = nat[:, :b_w].astype(o_ref.dtype)
            st_ref[rows, :] = (nat[:, b_w:b_w + LANES] + nat[:, b_w + LANES:b_w + 2 * LANES]
                               + nat[:, b_w + 2 * LANES:])


def _dilated_bias(slopes, dil):
    win = DIL_TU + 2 * DIL_RADIUS
    r = np.arange(win)[:, None]
    c = np.arange(DIL_TU)[None, :]
    du = np.stack([np.abs(c + off - r) for off in (0, DIL_RADIUS, 2 * DIL_RADIUS)])
    alibi = -(du * dil).astype(np.float32)[None] * slopes[:, None, None, None]
    table = np.where(du[None] <= DIL_RADIUS, alibi, np.float32(NEG_BIG))
    table = table.reshape(len(slopes) // 2, 2, 3, win, DIL_TU).transpose(0, 2, 3, 1, 4)
    return jnp.asarray(table.reshape(len(slopes) // 2, 3, win, 2 * DIL_TU), F32)


def _dilated(qc, kg, vg, slopes, dil):
    bn, seq, b_w = qc.shape
    u_len = seq // dil
    n_sub = max(1, DIL_TOKENS // (DIL_TU * dil))
    tl = n_sub * DIL_TU * dil
    per_iter = 8
    extra = LANES if dil == 1 else 3 * LANES
    n_dtype = F32 if dil == 1 else BF16
    bias = _dilated_bias(slopes, dil)
    inv = jnp.asarray(_chunk_perm(dil).T, BF16)
    o, st = pl.pallas_call(
        functools.partial(_dilated_kernel, dil=dil, u_len=u_len, n_sub=n_sub, per_iter=per_iter),
        out_shape=[jax.ShapeDtypeStruct((bn, seq, b_w), BF16),
                   jax.ShapeDtypeStruct((bn, seq, LANES), F32)],
        grid=(bn, seq // tl),
        in_specs=[_resident(bias.shape), _resident(inv.shape),
                  pl.BlockSpec((None, tl, b_w), lambda b, i: (b, i, 0)),
                  pl.BlockSpec((None, seq, b_w), lambda b, i: (b, 0, 0), pipeline_mode=pl.Buffered(1)),
                  pl.BlockSpec((None, seq, b_w), lambda b, i: (b, 0, 0), pipeline_mode=pl.Buffered(1))],
        out_specs=[pl.BlockSpec((None, tl, b_w), lambda b, i: (b, i, 0)),
                   pl.BlockSpec((None, tl, LANES), lambda b, i: (b, i, 0))],
        scratch_shapes=[pltpu.VMEM((tl, b_w + extra), n_dtype),
                        pltpu.VMEM((per_iter, LANES, DIL_TU), F32)],
        compiler_params=_params("parallel", "parallel"),
        name=f"dilated{dil}",
    )(bias, inv, qc, kg, vg)
    return o.reshape(bn * seq, b_w), st.reshape(bn * seq, LANES)


def _conv_kernel(prev_ref, cur_ref, next_ref, w_ref, b_ref, g_ref, o_ref, pad_ref, cv_ref, *, ts, tr):
    i = pl.program_id(1)
    n = pl.num_programs(1)
    prev = prev_ref[...]
    nxt = next_ref[...]
    pad_ref[0:CONV_HALO, :] = jnp.where(i > 0, prev, jnp.zeros_like(prev))
    pad_ref[CONV_HALO:CONV_HALO + ts, :] = cur_ref[...]
    pad_ref[CONV_HALO + ts:, :] = jnp.where(i < n - 1, nxt, jnp.zeros_like(nxt))
    base = CONV_HALO - CONV_W // 2
    wrows = tr + 2 * CONV_HALO

    def rows(r, carry):
        t0 = pl.multiple_of(r * tr, tr)
        for c in range(o_ref.shape[1] // LANES):
            cols = slice(c * LANES, (c + 1) * LANES)
            win = pad_ref[pl.ds(t0, wrows), cols]
            acc = jnp.zeros((tr, LANES), F32) + b_ref[:, cols]
            for b in range(SUBLANES):
                wb = win if b == 0 else pltpu.roll(win, shift=wrows - b, axis=0)
                for a in range(2 * CONV_HALO // SUBLANES):
                    tap = SUBLANES * a + b - base
                    if 0 <= tap < CONV_W:
                        acc = acc + wb[SUBLANES * a:SUBLANES * a + tr, :] * w_ref[tap:tap + 1, cols]
            cv_ref[pl.ds(t0, tr), cols] = acc
        y = _rms(cv_ref[pl.ds(t0, tr), :], g_ref[...])
        o_ref[pl.ds(t0, tr), :] = (y * jax.nn.sigmoid(y)).astype(o_ref.dtype)
        return carry

    lax.fori_loop(0, ts // tr, rows, 0)


def _conv(z, dw_w, dw_b, norm_g, *, ts=1024, tr=64):
    bn, seq, c = z.shape
    per = ts // CONV_HALO
    last = seq // CONV_HALO - 1
    return pl.pallas_call(
        functools.partial(_conv_kernel, ts=ts, tr=tr),
        out_shape=jax.ShapeDtypeStruct((bn, seq, c), BF16),
        grid=(bn, seq // ts),
        in_specs=[pl.BlockSpec((None, CONV_HALO, c), lambda b, i: (b, jnp.maximum(i * per - 1, 0), 0)),
                  pl.BlockSpec((None, ts, c), lambda b, i: (b, i, 0)),
                  pl.BlockSpec((None, CONV_HALO, c), lambda b, i: (b, jnp.minimum((i + 1) * per, last), 0)),
                  _resident(dw_w.shape), _resident((1, c)), _resident((1, c))],
        out_specs=pl.BlockSpec((None, ts, c), lambda b, i: (b, i, 0)),
        scratch_shapes=[pltpu.VMEM((ts + 2 * CONV_HALO, c), F32), pltpu.VMEM((ts, c), F32)],
        compiler_params=_params("parallel", "parallel"),
        name="conv",
    )(z, z, z, dw_w, dw_b, norm_g)


def _merge_kernel(x_ref, ya_ref, o1_ref, o2_ref, o3_ref, l1_ref, l2_ref, l3_ref, yc_ref, gate_ref,
                  exp_ref, wa_ref, wb_ref, wc_ref, wo_ref, out_ref, *, d):
    stats = (l1_ref[...], l2_ref[...], l3_ref[...])
    dens = [pltpu.roll(s, LANES - DEN_ROW, axis=1) for s in stats]
    top = jnp.maximum(jnp.maximum(stats[0], stats[1]), stats[2])
    es = [jnp.exp2(s - top) for s in stats]
    total = es[0] * dens[0] + es[1] * dens[1] + es[2] * dens[2]
    heads = lax.broadcasted_iota(jnp.int32, (1, LANES), 1) < exp_ref.shape[1] // HEAD_DIM
    inv = jnp.where(heads, 1.0 / total, 0.0)
    mix = None
    for e, o_ref in zip(es, (o1_ref, o2_ref, o3_ref)):
        w = e * inv
        wide = _dot(w.astype(BF16), exp_ref[...])
        term = wide * o_ref[...].astype(F32)
        mix = term if mix is None else mix + term
    slab = ya_ref.shape[1]
    ya = _dot_tn(ya_ref[0], wa_ref[0:slab, :])
    for n in range(1, ya_ref.shape[0]):
        ya = ya + _dot_tn(ya_ref[n], wa_ref[n * slab:(n + 1) * slab, :])
    yb = _dot(mix.astype(BF16), wb_ref[...])
    yc = _dot(yc_ref[...], wc_ref[...])
    merged = (gate_ref[:, 0:d].astype(F32) * ya + gate_ref[:, d:2 * d].astype(F32) * yb
              + gate_ref[:, 2 * d:3 * d].astype(F32) * yc)
    out_ref[...] = x_ref[...] + _dot(merged.astype(BF16), wo_ref[...])


def _merge(x2, yat, outs, lses, yc, gates, expand, wa, wb, wc, wo, *, tm=512):
    t, d = x2.shape
    row = lambda i: (i, 0)
    per_batch = yat.shape[3] // tm
    tiles = [x2, *outs, *lses, yc, gates]
    weights = [expand, wa, wb, wc, wo]
    specs = [pl.BlockSpec((tm, a.shape[1]), row) for a in tiles]
    specs.insert(1, pl.BlockSpec((None,) + yat.shape[1:3] + (tm,),
                                 lambda i: (i // per_batch, 0, 0, i % per_batch)))
    tiles.insert(1, yat)
    return pl.pallas_call(
        functools.partial(_merge_kernel, d=d),
        out_shape=jax.ShapeDtypeStruct((t, d), F32),
        grid=(t // tm,),
        in_specs=specs + [_resident(w.shape) for w in weights],
        out_specs=pl.BlockSpec((tm, d), row),
        compiler_params=_params("parallel"),
        name="merge",
    )(*tiles, *weights)


def _alibi_slopes(n):
    return (2.0 ** (-8.0 * np.arange(1, n + 1) / n) * LOG2E).astype(np.float32)


def _pair_gain(g, scale=1.0):
    return (jnp.concatenate([g, g]) * scale).reshape(1, LANES).astype(F32)


def kernel(x, ffn1_norm, ffn1_w_up, ffn1_w_down, mix_norm, w_in, b_in, a_q_norm, a_k_norm, a_lambda, a_sub_norm, w_out_a, b_q_norm, b_k_norm, w_out_b, c_dw_w, c_dw_b, c_norm, w_out_c, w_out, ffn2_norm, ffn2_w_up, ffn2_w_down):
    bn, seq, d = x.shape
    depth = w_in.shape[0]
    a_w = w_out_a.shape[1]
    b_w = w_out_b.shape[1]
    c_ch = w_out_c.shape[1]
    sizes = (a_w, a_w, a_w, b_w, b_w, b_w, 2 * c_ch, 3 * d)
    slopes_a = _alibi_slopes(A_HEADS)
    slopes_b = _alibi_slopes(B_HEADS)
    expand = (jnp.arange(LANES)[:, None] == (jnp.arange(b_w)[None, :] // HEAD_DIM)).astype(BF16)
    row = lambda v: v.reshape(1, -1).astype(F32)

    x2 = x.reshape(bn * seq, d)
    for l in range(depth):
        x2 = _ffn(x2, row(ffn1_norm[l]), ffn1_w_up[l].astype(BF16), ffn1_w_down[l].astype(BF16))

        dils = tuple(dil for _, dil in DILATED_PATTERNS if dil > 1)
        aqt, ak, av, bq, bk, bv, z, gates, *regrouped = _inproj(
            x2, row(mix_norm[l]), w_in[l].astype(BF16), row(b_in[l]),
            _pair_gain(a_q_norm[l], ATTN_SCALE * LOG2E), _pair_gain(a_k_norm[l]),
            _pair_gain(b_q_norm[l], ATTN_SCALE * LOG2E), _pair_gain(b_k_norm[l]), sizes, bn, dils)

        lam0 = 0.8 - 0.6 * math.exp(-0.3 * l)
        to3 = lambda t: t.reshape(bn, seq, t.shape[-1])
        yat = _diffattn(aqt, to3(ak), to3(av), slopes_a, a_lambda[l].astype(F32),
                        a_sub_norm[l].reshape(-1, 1).astype(F32), lam0=lam0)
        yat = yat.reshape(bn, A_HEADS // 2, 2 * LANES, seq)

        outs, lses = [], []
        for _, dil in DILATED_PATTERNS:
            qkv = (bq, bk, bv) if dil == 1 else regrouped[3 * dils.index(dil):3 * dils.index(dil) + 3]
            o, lse = _dilated(*(to3(t) for t in qkv), slopes_b, dil)
            outs.append(o)
            lses.append(lse)

        yc = _conv(to3(z), c_dw_w[l].astype(F32), row(c_dw_b[l]), row(c_norm[l])).reshape(bn * seq, c_ch)

        x2 = _merge(x2, yat, outs, lses, yc, gates, expand,
                    w_out_a[l].astype(BF16), w_out_b[l].astype(BF16), w_out_c[l].astype(BF16),
                    w_out[l].astype(BF16))

        x2 = _ffn(x2, row(ffn2_norm[l]), ffn2_w_up[l].astype(BF16), ffn2_w_down[l].astype(BF16))
    return x2.reshape(bn, seq, d)
```

```python
import functools
import math

import jax
import jax.numpy as jnp
import numpy as np
from jax import lax
from jax.experimental import pallas as pl
from jax.experimental.pallas import tpu as pltpu

HEAD_DIM = 64
LANES = 128
SUBLANES = 8
BF16_ROWS = 16
A_HEADS = 8
B_HEADS = 12
CONV_W = 31
CONV_HALO = 16
DIL_RADIUS = 64
DIL_TU = 128
DIL_TOKENS = 1024
CHUNK = 256
DEN_ROW = 16
DILATED_PATTERNS = ((128, 1), (512, 4), (2048, 16))
ATTN_SCALE = HEAD_DIM ** -0.5
LOG2E = math.log2(math.e)
EPS = 1e-6
NEG_BIG = -1e30
VMEM_LIMIT = 56 * 1024 * 1024

BF16 = jnp.bfloat16
F32 = jnp.float32


def _params(*sem):
    return pltpu.CompilerParams(dimension_semantics=sem, vmem_limit_bytes=VMEM_LIMIT)


def _resident(shape):
    nd = len(shape)
    return pl.BlockSpec(shape, lambda *_: (0,) * nd, pipeline_mode=pl.Buffered(1))


def _rms(xf, g):
    ms = jnp.mean(xf * xf, axis=-1, keepdims=True)
    return xf * lax.rsqrt(ms + EPS) * g


def _dot(a, b):
    return jnp.dot(a, b, preferred_element_type=F32)


def _dot_nt(a, b):
    return lax.dot_general(a, b, (((1,), (1,)), ((), ())), preferred_element_type=F32)


def _dot_tn(a, b):
    return lax.dot_general(a, b, (((0,), (0,)), ((), ())), preferred_element_type=F32)


def _ffn_kernel(x_ref, g_ref, wu_ref, wd_ref, o_ref, *, d_ff, n_chunks):
    x = x_ref[...]
    h = _rms(x, g_ref[...]).astype(BF16)
    ck = d_ff // n_chunks
    y = jnp.zeros_like(x)
    for c in range(n_chunks):
        a = _dot(h, wu_ref[:, c * ck:(c + 1) * ck])
        b = _dot(h, wu_ref[:, d_ff + c * ck:d_ff + (c + 1) * ck])
        act = (a * jax.nn.sigmoid(a) * b).astype(BF16)
        y = y + _dot(act, wd_ref[c * ck:(c + 1) * ck, :])
    o_ref[...] = x + 0.5 * y


def _ffn(x2, g, w_up, w_down, *, tm=1024):
    t, d = x2.shape
    d_ff = w_down.shape[0]
    return pl.pallas_call(
        functools.partial(_ffn_kernel, d_ff=d_ff, n_chunks=2),
        out_shape=jax.ShapeDtypeStruct((t, d), F32),
        grid=(t // tm,),
        in_specs=[pl.BlockSpec((tm, d), lambda i: (i, 0)),
                  _resident((1, d)), _resident(w_up.shape), _resident(w_down.shape)],
        out_specs=pl.BlockSpec((tm, d), lambda i: (i, 0)),
        compiler_params=_params("parallel"),
        name="ffn",
    )(x2, g, w_up, w_down)


def _headnorm_store(y, g128, o_ref, transposed=False):
    lane = lax.broadcasted_iota(jnp.int32, (1, LANES), 1)
    lo = lane < HEAD_DIM
    for c in range(y.shape[1] // LANES):
        blk = y[:, c * LANES:(c + 1) * LANES]
        sq = blk * blk
        s_lo = jnp.sum(jnp.where(lo, sq, 0.0), axis=-1, keepdims=True)
        s_hi = jnp.sum(jnp.where(lo, 0.0, sq), axis=-1, keepdims=True)
        ms = jnp.where(lo, s_lo, s_hi) * (1.0 / HEAD_DIM)
        normed = blk * lax.rsqrt(ms + EPS) * g128
        if transposed:
            o_ref[c] = normed.T.astype(o_ref.dtype)
        else:
            o_ref[:, c * LANES:(c + 1) * LANES] = normed.astype(o_ref.dtype)


def _inproj_kernel(x_ref, g_ref, w_ref, b_ref, aqg_ref, akg_ref, bqg_ref, bkg_ref, perm_ref,
                   aq_o, ak_o, av_o, bq_o, bk_o, bv_o, z_o, gate_o, *regrouped, splits, dils):
    h = _rms(x_ref[...], g_ref[...]).astype(BF16)

    def proj(n):
        lo, hi = splits[n], splits[n + 1]
        return _dot(h, w_ref[:, lo:hi]) + b_ref[:, lo:hi]

    _headnorm_store(proj(0), aqg_ref[...], aq_o, transposed=True)
    _headnorm_store(proj(1), akg_ref[...], ak_o)
    av_o[...] = proj(2).astype(av_o.dtype)
    _headnorm_store(proj(3), bqg_ref[...], bq_o)
    _headnorm_store(proj(4), bkg_ref[...], bk_o)
    bv_o[...] = proj(5).astype(bv_o.dtype)
    cu = proj(6)
    c_ch = cu.shape[1] // 2
    z_o[...] = (cu[:, :c_ch] * jax.nn.sigmoid(cu[:, c_ch:])).astype(z_o.dtype)
    gate_o[...] = jax.nn.sigmoid(proj(7)).astype(gate_o.dtype)

    for n, dil in enumerate(dils):
        perm = perm_ref[n]
        qc_o, kg_o, vg_o = regrouped[3 * n:3 * n + 3]
        qc_o[...] = _dot(perm, bq_o[...]).astype(qc_o.dtype)
        group = CHUNK // dil
        for src, dst in ((bk_o, kg_o), (bv_o, vg_o)):
            moved = _dot(perm, src[...]).astype(dst.dtype)
            for r in range(dil):
                dst[r] = moved[r * group:(r + 1) * group, :]


def _chunk_perm(dil):
    src = np.arange(CHUNK)
    dst = (src % dil) * (CHUNK // dil) + src // dil
    perm = np.zeros((CHUNK, CHUNK), np.float32)
    perm[dst, src] = 1.0
    return perm


def _inproj(x2, g, w_in, b_in, aqg, akg, bqg, bkg, sizes, bn, dils):
    t, d = x2.shape
    tm = CHUNK
    n_chunks = t // bn // CHUNK
    splits = tuple(int(v) for v in np.concatenate([[0], np.cumsum(sizes)]))
    a_qk, _, a_v, b_w, _, _, c2, gate_w = sizes
    widths = (a_qk, a_qk, a_v, b_w, b_w, b_w, c2 // 2, gate_w)
    dtypes = (BF16, BF16, BF16, BF16, BF16, BF16, F32, BF16)
    row = lambda i: (i, 0)
    out_shape = [jax.ShapeDtypeStruct((t, w), dt) for w, dt in zip(widths, dtypes)]
    out_specs = [pl.BlockSpec((tm, w), row) for w in widths]
    out_shape[0] = jax.ShapeDtypeStruct((bn, a_qk // LANES, LANES, t // bn), BF16)
    out_specs[0] = pl.BlockSpec((None, a_qk // LANES, LANES, tm),
                                lambda i: (i // n_chunks, 0, 0, i % n_chunks))
    for dil in dils:
        group = CHUNK // dil
        grouped = jax.ShapeDtypeStruct((bn, dil, n_chunks, group, b_w), BF16)
        grouped_spec = pl.BlockSpec((None, dil, None, group, b_w),
                                    lambda i: (i // n_chunks, 0, i % n_chunks, 0, 0))
        out_shape += [jax.ShapeDtypeStruct((t, b_w), BF16), grouped, grouped]
        out_specs += [pl.BlockSpec((tm, b_w), row), grouped_spec, grouped_spec]
    perms = jnp.asarray(np.stack([_chunk_perm(dil) for dil in dils]), BF16)
    return pl.pallas_call(
        functools.partial(_inproj_kernel, splits=splits, dils=dils),
        out_shape=out_shape,
        grid=(t // tm,),
        in_specs=[pl.BlockSpec((tm, d), row), _resident((1, d)), _resident(w_in.shape),
                  _resident(b_in.shape)] + [_resident((1, LANES))] * 4 + [_resident(perms.shape)],
        out_specs=out_specs,
        compiler_params=_params("parallel"),
        name="inproj",
    )(x2, g, w_in, b_in, aqg, akg, bqg, bkg, perms)


def _diffattn_kernel(slopes_ref, lam_ref, gsub_ref, qb_ref, kb_ref, q_ref, k_ref, v_ref, o_ref,
                     vt_ref, t_ref, s_ref, m_ref, acc_ref, qt_ref, top_ref, *, tq, seq, lam0, n_sub):
    hd = pl.program_id(1)
    n_tiles = seq // tq
    sig = slopes_ref[hd]

    @pl.when(pl.program_id(2) == 0)
    def _():
        for c in range(n_tiles):
            vt_ref[c, :LANES, :] = v_ref[c * tq:(c + 1) * tq, :].astype(F32).T.astype(BF16)
            vt_ref[c, LANES:, :] = jnp.ones((BF16_ROWS, tq), BF16)
        t = -jnp.abs(lax.broadcasted_iota(jnp.int32, (tq, tq), 0)
                     - lax.broadcasted_iota(jnp.int32, (tq, tq), 1)).astype(F32) * sig
        t_ref[:, :tq] = t
        t_ref[:, tq:] = t

    for sub in range(n_sub):
        _diffattn_query_tile(pl.program_id(2) * n_sub + sub, sig, lam_ref, gsub_ref, qb_ref, kb_ref,
                             q_ref.at[:, sub * tq:(sub + 1) * tq], k_ref,
                             o_ref.at[:, sub * tq:(sub + 1) * tq], vt_ref, t_ref, s_ref,
                             m_ref.at[sub], acc_ref.at[sub], qt_ref.at[sub], top_ref,
                             tq=tq, n_tiles=n_tiles, lam0=lam0)


def _diffattn_query_tile(qi, sig, lam_ref, gsub_ref, qb_ref, kb_ref, q_ref, k_ref, o_ref, vt_ref,
                         t_ref, s_ref, m_ref, acc_ref, qt_ref, top_ref, *, tq, n_tiles, lam0):
    zero = jnp.zeros((HEAD_DIM, tq), BF16)
    qt_ref[:HEAD_DIM, :tq] = q_ref[:HEAD_DIM, :]
    qt_ref[HEAD_DIM:LANES, :tq] = zero
    qt_ref[:HEAD_DIM, tq:] = zero
    qt_ref[HEAD_DIM:LANES, tq:] = q_ref[HEAD_DIM:, :]
    qt_ref[LANES:, :tq] = qb_ref[...]
    qt_ref[LANES:, tq:] = qb_ref[...]

    def key_tile(j):
        return k_ref[pl.ds(pl.multiple_of(j * tq, tq), tq), :]

    def tile_index(jj):
        return jnp.where(jj == 0, qi, jj - (jj <= qi).astype(jnp.int32))

    def put_scores(slot, st):
        s_ref[slot] = st
        top_ref[slot] = jnp.max(st, axis=0, keepdims=True)

    def scores(jj, slot):
        j = tile_index(jj)
        after = (j > qi).astype(jnp.int32)
        put_scores(slot, _dot(jnp.concatenate([key_tile(j), kb_ref[after]], axis=1), qt_ref[...]))

    def absorb(jj, slot):
        j = tile_index(jj)
        far = (jnp.abs(qi - j) * tq).astype(F32) * sig
        m = m_ref[...]
        m_new = jnp.maximum(m, top_ref[slot] - far)
        alpha = jnp.exp2(m - m_new)
        p = jnp.exp2(s_ref[slot] - (m_new + far)).astype(BF16)
        m_ref[...] = m_new
        acc_ref[...] = alpha * acc_ref[...] + _dot(vt_ref[j], p)

    m_ref[...] = jnp.full(m_ref.shape, NEG_BIG, F32)
    acc_ref[...] = jnp.zeros(acc_ref.shape, F32)
    put_scores(0, _dot(key_tile(qi), qt_ref[:LANES, :]) + t_ref[...])

    scores(1, 1)
    absorb(0, 0)

    def triple(i, carry):
        for n in range(1, 4):
            scores(3 * i + n + 1, (n + 1) % 3)
            absorb(3 * i + n, n % 3)
        return carry

    lax.fori_loop(0, (n_tiles - 2) // 3, triple, 0)
    absorb(n_tiles - 1, (n_tiles - 1) % 3)
    acc = acc_ref[...]

    lp = lam_ref[...]
    d1 = jnp.sum(lp[0:1] * lp[1:2], axis=-1, keepdims=True)
    d2 = jnp.sum(lp[2:3] * lp[3:4], axis=-1, keepdims=True)
    lam = jnp.exp(d1) - jnp.exp(d2) + lam0
    yt = acc[:LANES] / acc[LANES:LANES + 1]
    yt = yt[:, :tq] - lam * yt[:, tq:]
    ms = jnp.mean(yt * yt, axis=0, keepdims=True)
    o_ref[...] = (yt * lax.rsqrt(ms + EPS) * (gsub_ref[...] * (1.0 - lam0))).astype(o_ref.dtype)


def _split3(x):
    p1 = x.astype(BF16)
    r1 = x - p1.astype(F32)
    p2 = r1.astype(BF16)
    p3 = (r1 - p2.astype(F32)).astype(BF16)
    return [p1, p2, p3]


def _alibi_lanes(slopes, tq):
    pos = (slopes[:, None] * np.arange(tq, dtype=np.float32)[None, :]).astype(np.float32)
    one = np.ones_like(pos)
    pad = np.zeros(pos.shape + (LANES - 6,), np.float32)
    stack = lambda cols: np.concatenate([np.stack(cols, axis=-1), pad], axis=-1)
    pieces = [np.asarray(p, np.float32) for p in _split3(pos)]
    q_side = stack([one, one, one] + pieces)
    k_side = stack(pieces + [-one, -one, -one])
    return (jnp.asarray(q_side.transpose(0, 2, 1), BF16),
            jnp.asarray(np.stack([k_side, -k_side], axis=1), BF16))


def _diffattn(aqt, ak, av, slopes, lam_p, gsub, *, lam0, tq=512, n_sub=2):
    bn, seq, _ = ak.shape
    smem = pl.BlockSpec(memory_space=pltpu.SMEM)
    qb, kb = _alibi_lanes(slopes, tq)
    return pl.pallas_call(
        functools.partial(_diffattn_kernel, tq=tq, seq=seq, lam0=lam0, n_sub=n_sub),
        out_shape=jax.ShapeDtypeStruct(aqt.shape, BF16),
        grid=(bn, A_HEADS, seq // (tq * n_sub)),
        in_specs=[smem, _resident(lam_p.shape), _resident((LANES, 1)),
                  pl.BlockSpec((None, LANES, tq), lambda b, h, i: (h, 0, 0)),
                  pl.BlockSpec((None, 2, tq, LANES), lambda b, h, i: (h, 0, 0, 0)),
                  pl.BlockSpec((None, None, LANES, tq * n_sub), lambda b, h, i: (b, h, 0, i)),
                  pl.BlockSpec((None, seq, LANES), lambda b, h, i: (b, 0, h)),
                  pl.BlockSpec((None, seq, LANES), lambda b, h, i: (b, 0, h))],
        out_specs=pl.BlockSpec((None, None, LANES, tq * n_sub), lambda b, h, i: (b, h, 0, i)),
        scratch_shapes=[pltpu.VMEM((seq // tq, LANES + BF16_ROWS, tq), BF16),
                        pltpu.VMEM((tq, 2 * tq), F32),
                        pltpu.VMEM((3, tq, 2 * tq), F32),
                        pltpu.VMEM((n_sub, 1, 2 * tq), F32),
                        pltpu.VMEM((n_sub, LANES + BF16_ROWS, 2 * tq), F32),
                        pltpu.VMEM((n_sub, 2 * LANES, 2 * tq), BF16),
                        pltpu.VMEM((3, 1, 2 * tq), F32)],
        compiler_params=_params("parallel", "parallel", "arbitrary"),
        name="diffattn",
    )(jnp.asarray(slopes), lam_p, gsub, qb, kb, aqt, ak, av)


def _dilated_kernel(bias_ref, inv_ref, q_ref, k_ref, v_ref, o_ref, st_ref, n_ref, rows_ref, *,
                    dil, u_len, n_sub, per_iter):
    b_w = o_ref.shape[1]
    n_pairs = b_w // LANES
    win = DIL_TU + 2 * DIL_RADIUS
    group = CHUNK // dil if dil > 1 else DIL_TU
    n_slabs = DIL_TU // group
    lane = lax.broadcasted_iota(jnp.int32, (1, LANES), 1)
    lo = lane < HEAD_DIM
    rows_ref[...] = jnp.zeros(rows_ref.shape, F32)

    def tile(t, rows_ref):
        r = t % dil
        sub = t // dil
        u0 = (pl.program_id(1) * n_sub + sub) * DIL_TU
        start = pl.multiple_of(jnp.clip(u0 - DIL_RADIUS, 0, u_len - win), DIL_RADIUS)
        variant = (u0 - start) // DIL_RADIUS
        base = pl.multiple_of(r * u_len + start, DIL_RADIUS)

        def slab(c):
            row = (sub * n_slabs + c) * (CHUNK if dil > 1 else DIL_TU) + r * group
            return pl.ds(pl.multiple_of(row, group), group)

        for pair in range(n_pairs):
            cols = slice(pair * LANES, (pair + 1) * LANES)
            q = jnp.concatenate([q_ref[slab(c), cols] for c in range(n_slabs)], axis=0)
            zero = jnp.zeros_like(q)
            qcat = jnp.concatenate([jnp.where(lo, q, zero), jnp.where(lo, zero, q)], axis=0)
            st = _dot_nt(k_ref[pl.ds(base, win), cols], qcat) + bias_ref[pair, variant]
            m = jnp.max(st, axis=0, keepdims=True)
            p = jnp.exp2(st - m)
            rows_ref[2 * pair:2 * pair + 1, :] = m[:, :DIL_TU]
            rows_ref[2 * pair + 1:2 * pair + 2, :] = m[:, DIL_TU:]
            den = jnp.sum(p, axis=0, keepdims=True)
            rows_ref[DEN_ROW + 2 * pair:DEN_ROW + 2 * pair + 1, :] = den[:, :DIL_TU]
            rows_ref[DEN_ROW + 2 * pair + 1:DEN_ROW + 2 * pair + 2, :] = den[:, DIL_TU:]
            pb = p.astype(BF16)
            vw = v_ref[pl.ds(base, win), cols]
            o0 = _dot_tn(pb[:, :DIL_TU], vw)
            o1 = _dot_tn(pb[:, DIL_TU:], vw)
            out = jnp.where(lo, o0, o1)
            for c in range(n_slabs):
                n_ref[slab(c), cols] = out[c * group:(c + 1) * group, :].astype(n_ref.dtype)
        stats = rows_ref[...].T
        pieces = [stats] if dil == 1 else _split3(stats)
        for n, piece in enumerate(pieces):
            cols = slice(b_w + n * LANES, b_w + (n + 1) * LANES)
            for c in range(n_slabs):
                n_ref[slab(c), cols] = piece[c * group:(c + 1) * group, :].astype(n_ref.dtype)

    def tiles(it, carry):
        for n in range(per_iter):
            tile(it * per_iter + n, rows_ref.at[n])
        return carry

    lax.fori_loop(0, n_sub * dil // per_iter, tiles, 0)

    if dil == 1:
        o_ref[...] = n_ref[:, :b_w].astype(o_ref.dtype)
        st_ref[...] = n_ref[:, b_w:]
    else:
        for c in range(o_ref.shape[0] // CHUNK):
            rows = slice(c * CHUNK, (c + 1) * CHUNK)
            nat = _dot(inv_ref[...], n_ref[rows, :])
            o_ref[rows, :] = nat[:, :b_w].astype(o_ref.dtype)
            st_ref[rows, :] = (nat[:, b_w:b_w + LANES] + nat[:, b_w + LANES:b_w + 2 * LANES]
                               + nat[:, b_w + 2 * LANES:])


def _dilated_bias(slopes, dil):
    win = DIL_TU + 2 * DIL_RADIUS
    r = np.arange(win)[:, None]
    c = np.arange(DIL_TU)[None, :]
    du = np.stack([np.abs(c + off - r) for off in (0, DIL_RADIUS, 2 * DIL_RADIUS)])
    alibi = -(du * dil).astype(np.float32)[None] * slopes[:, None, None, None]
    table = np.where(du[None] <= DIL_RADIUS, alibi, np.float32(NEG_BIG))
    table = table.reshape(len(slopes) // 2, 2, 3, win, DIL_TU).transpose(0, 2, 3, 1, 4)
    return jnp.asarray(table.reshape(len(slopes) // 2, 3, win, 2 * DIL_TU), F32)


def _dilated(qc, kg, vg, slopes, dil):
    bn, seq, b_w = qc.shape
    u_len = seq // dil
    n_sub = max(1, DIL_TOKENS // (DIL_TU * dil))
    tl = n_sub * DIL_TU * dil
    per_iter = 8
    extra = LANES if dil == 1 else 3 * LANES
    n_dtype = F32 if dil == 1 else BF16
    bias = _dilated_bias(slopes, dil)
    inv = jnp.asarray(_chunk_perm(dil).T, BF16)
    o, st = pl.pallas_call(
        functools.partial(_dilated_kernel, dil=dil, u_len=u_len, n_sub=n_sub, per_iter=per_iter),
        out_shape=[jax.ShapeDtypeStruct((bn, seq, b_w), BF16),
                   jax.ShapeDtypeStruct((bn, seq, LANES), F32)],
        grid=(bn, seq // tl),
        in_specs=[_resident(bias.shape), _resident(inv.shape),
                  pl.BlockSpec((None, tl, b_w), lambda b, i: (b, i, 0)),
                  pl.BlockSpec((None, seq, b_w), lambda b, i: (b, 0, 0), pipeline_mode=pl.Buffered(1)),
                  pl.BlockSpec((None, seq, b_w), lambda b, i: (b, 0, 0), pipeline_mode=pl.Buffered(1))],
        out_specs=[pl.BlockSpec((None, tl, b_w), lambda b, i: (b, i, 0)),
                   pl.BlockSpec((None, tl, LANES), lambda b, i: (b, i, 0))],
        scratch_shapes=[pltpu.VMEM((tl, b_w + extra), n_dtype),
                        pltpu.VMEM((per_iter, LANES, DIL_TU), F32)],
        compiler_params=_params("parallel", "parallel"),
        name=f"dilated{dil}",
    )(bias, inv, qc, kg, vg)
    return o.reshape(bn * seq, b_w), st.reshape(bn * seq, LANES)


def _conv_kernel(prev_ref, cur_ref, next_ref, w_ref, b_ref, g_ref, o_ref, pad_ref, cv_ref, *, ts, tr):
    i = pl.program_id(1)
    n = pl.num_programs(1)
    prev = prev_ref[...]
    nxt = next_ref[...]
    pad_ref[0:CONV_HALO, :] = jnp.where(i > 0, prev, jnp.zeros_like(prev))
    pad_ref[CONV_HALO:CONV_HALO + ts, :] = cur_ref[...]
    pad_ref[CONV_HALO + ts:, :] = jnp.where(i < n - 1, nxt, jnp.zeros_like(nxt))
    base = CONV_HALO - CONV_W // 2
    wrows = tr + 2 * CONV_HALO

    def rows(r, carry):
        t0 = pl.multiple_of(r * tr, tr)
        for c in range(o_ref.shape[1] // LANES):
            cols = slice(c * LANES, (c + 1) * LANES)
            win = pad_ref[pl.ds(t0, wrows), cols]
            acc = jnp.zeros((tr, LANES), F32) + b_ref[:, cols]
            for b in range(SUBLANES):
                wb = win if b == 0 else pltpu.roll(win, shift=wrows - b, axis=0)
                for a in range(2 * CONV_HALO // SUBLANES):
                    tap = SUBLANES * a + b - base
                    if 0 <= tap < CONV_W:
                        acc = acc + wb[SUBLANES * a:SUBLANES * a + tr, :] * w_ref[tap:tap + 1, cols]
            cv_ref[pl.ds(t0, tr), cols] = acc
        y = _rms(cv_ref[pl.ds(t0, tr), :], g_ref[...])
        o_ref[pl.ds(t0, tr), :] = (y * jax.nn.sigmoid(y)).astype(o_ref.dtype)
        return carry

    lax.fori_loop(0, ts // tr, rows, 0)


def _conv(z, dw_w, dw_b, norm_g, *, ts=1024, tr=128):
    bn, seq, c = z.shape
    per = ts // CONV_HALO
    last = seq // CONV_HALO - 1
    return pl.pallas_call(
        functools.partial(_conv_kernel, ts=ts, tr=tr),
        out_shape=jax.ShapeDtypeStruct((bn, seq, c), BF16),
        grid=(bn, seq // ts),
        in_specs=[pl.BlockSpec((None, CONV_HALO, c), lambda b, i: (b, jnp.maximum(i * per - 1, 0), 0)),
                  pl.BlockSpec((None, ts, c), lambda b, i: (b, i, 0)),
                  pl.BlockSpec((None, CONV_HALO, c), lambda b, i: (b, jnp.minimum((i + 1) * per, last), 0)),
                  _resident(dw_w.shape), _resident((1, c)), _resident((1, c))],
        out_specs=pl.BlockSpec((None, ts, c), lambda b, i: (b, i, 0)),
        scratch_shapes=[pltpu.VMEM((ts + 2 * CONV_HALO, c), F32), pltpu.VMEM((ts, c), F32)],
        compiler_params=_params("parallel", "parallel"),
        name="conv",
    )(z, z, z, dw_w, dw_b, norm_g)


def _merge_kernel(x_ref, ya_ref, o1_ref, o2_ref, o3_ref, l1_ref, l2_ref, l3_ref, yc_ref, gate_ref,
                  exp_ref, wa_ref, wb_ref, wc_ref, wo_ref, out_ref, *, d):
    stats = (l1_ref[...], l2_ref[...], l3_ref[...])
    dens = [pltpu.roll(s, LANES - DEN_ROW, axis=1) for s in stats]
    top = jnp.maximum(jnp.maximum(stats[0], stats[1]), stats[2])
    es = [jnp.exp2(s - top) for s in stats]
    total = es[0] * dens[0] + es[1] * dens[1] + es[2] * dens[2]
    heads = lax.broadcasted_iota(jnp.int32, (1, LANES), 1) < exp_ref.shape[1] // HEAD_DIM
    inv = jnp.where(heads, 1.0 / total, 0.0)
    mix = None
    for e, o_ref in zip(es, (o1_ref, o2_ref, o3_ref)):
        w = e * inv
        wide = _dot(w.astype(BF16), exp_ref[...])
        term = wide * o_ref[...].astype(F32)
        mix = term if mix is None else mix + term
    slab = ya_ref.shape[1]
    ya = _dot_tn(ya_ref[0], wa_ref[0:slab, :])
    for n in range(1, ya_ref.shape[0]):
        ya = ya + _dot_tn(ya_ref[n], wa_ref[n * slab:(n + 1) * slab, :])
    yb = _dot(mix.astype(BF16), wb_ref[...])
    yc = _dot(yc_ref[...], wc_ref[...])
    merged = (gate_ref[:, 0:d].astype(F32) * ya + gate_ref[:, d:2 * d].astype(F32) * yb
              + gate_ref[:, 2 * d:3 * d].astype(F32) * yc)
    out_ref[...] = x_ref[...] + _dot(merged.astype(BF16), wo_ref[...])


def _merge(x2, yat, outs, lses, yc, gates, expand, wa, wb, wc, wo, *, tm=512):
    t, d = x2.shape
    row = lambda i: (i, 0)
    per_batch = yat.shape[3] // tm
    tiles = [x2, *outs, *lses, yc, gates]
    weights = [expand, wa, wb, wc, wo]
    specs = [pl.BlockSpec((tm, a.shape[1]), row) for a in tiles]
    specs.insert(1, pl.BlockSpec((None,) + yat.shape[1:3] + (tm,),
                                 lambda i: (i // per_batch, 0, 0, i % per_batch)))
    tiles.insert(1, yat)
    return pl.pallas_call(
        functools.partial(_merge_kernel, d=d),
        out_shape=jax.ShapeDtypeStruct((t, d), F32),
        grid=(t // tm,),
        in_specs=specs + [_resident(w.shape) for w in weights],
        out_specs=pl.BlockSpec((tm, d), row),
        compiler_params=_params("parallel"),
        name="merge",
    )(*tiles, *weights)


def _alibi_slopes(n):
    return (2.0 ** (-8.0 * np.arange(1, n + 1) / n) * LOG2E).astype(np.float32)


def _pair_gain(g, scale=1.0):
    return (jnp.concatenate([g, g]) * scale).reshape(1, LANES).astype(F32)


def kernel(x, ffn1_norm, ffn1_w_up, ffn1_w_down, mix_norm, w_in, b_in, a_q_norm, a_k_norm, a_lambda, a_sub_norm, w_out_a, b_q_norm, b_k_norm, w_out_b, c_dw_w, c_dw_b, c_norm, w_out_c, w_out, ffn2_norm, ffn2_w_up, ffn2_w_down):
    bn, seq, d = x.shape
    depth = w_in.shape[0]
    a_w = w_out_a.shape[1]
    b_w = w_out_b.shape[1]
    c_ch = w_out_c.shape[1]
    sizes = (a_w, a_w, a_w, b_w, b_w, b_w, 2 * c_ch, 3 * d)
    slopes_a = _alibi_slopes(A_HEADS)
    slopes_b = _alibi_slopes(B_HEADS)
    expand = (jnp.arange(LANES)[:, None] == (jnp.arange(b_w)[None, :] // HEAD_DIM)).astype(BF16)
    row = lambda v: v.reshape(1, -1).astype(F32)

    x2 = x.reshape(bn * seq, d)
    for l in range(depth):
        x2 = _ffn(x2, row(ffn1_norm[l]), ffn1_w_up[l].astype(BF16), ffn1_w_down[l].astype(BF16))

        dils = tuple(dil for _, dil in DILATED_PATTERNS if dil > 1)
        aqt, ak, av, bq, bk, bv, z, gates, *regrouped = _inproj(
            x2, row(mix_norm[l]), w_in[l].astype(BF16), row(b_in[l]),
            _pair_gain(a_q_norm[l], ATTN_SCALE * LOG2E), _pair_gain(a_k_norm[l]),
            _pair_gain(b_q_norm[l], ATTN_SCALE * LOG2E), _pair_gain(b_k_norm[l]), sizes, bn, dils)

        lam0 = 0.8 - 0.6 * math.exp(-0.3 * l)
        to3 = lambda t: t.reshape(bn, seq, t.shape[-1])
        yat = _diffattn(aqt, to3(ak), to3(av), slopes_a, a_lambda[l].astype(F32),
                        a_sub_norm[l].reshape(-1, 1).astype(F32), lam0=lam0)
        yat = yat.reshape(bn, A_HEADS // 2, 2 * LANES, seq)

        outs, lses = [], []
        for _, dil in DILATED_PATTERNS:
            qkv = (bq, bk, bv) if dil == 1 else regrouped[3 * dils.index(dil):3 * dils.index(dil) + 3]
            o, lse = _dilated(*(to3(t) for t in qkv), slopes_b, dil)
            outs.append(o)
            lses.append(lse)

        yc = _conv(to3(z), c_dw_w[l].astype(F32), row(c_dw_b[l]), row(c_norm[l])).reshape(bn * seq, c_ch)

        x2 = _merge(x2, yat, outs, lses, yc, gates, expand,
                    w_out_a[l].astype(BF16), w_out_b[l].astype(BF16), w_out_c[l].astype(BF16),
                    w_out[l].astype(BF16))

        x2 = _ffn(x2, row(ffn2_norm[l]), ffn2_w_up[l].astype(BF16), ffn2_w_down[l].astype(BF16))
    return x2.reshape(bn, seq, d)
```

```python
import functools
import math

import jax
import jax.numpy as jnp
import numpy as np
from jax import lax
from jax.experimental import pallas as pl
from jax.experimental.pallas import tpu as pltpu

HEAD_DIM = 64
LANES = 128
SUBLANES = 8
BF16_ROWS = 16
A_HEADS = 8
B_HEADS = 12
CONV_W = 31
CONV_HALO = 16
DIL_RADIUS = 64
DIL_TU = 128
DIL_TOKENS = 2048
CHUNK = 256
DEN_ROW = 16
DILATED_PATTERNS = ((128, 1), (512, 4), (2048, 16))
ATTN_SCALE = HEAD_DIM ** -0.5
LOG2E = math.log2(math.e)
EPS = 1e-6
NEG_BIG = -1e30
VMEM_LIMIT = 56 * 1024 * 1024

BF16 = jnp.bfloat16
F32 = jnp.float32


def _params(*sem):
    return pltpu.CompilerParams(dimension_semantics=sem, vmem_limit_bytes=VMEM_LIMIT)


def _resident(shape):
    nd = len(shape)
    return pl.BlockSpec(shape, lambda *_: (0,) * nd, pipeline_mode=pl.Buffered(1))


def _rms(xf, g):
    ms = jnp.mean(xf * xf, axis=-1, keepdims=True)
    return xf * lax.rsqrt(ms + EPS) * g


def _dot(a, b):
    return jnp.dot(a, b, preferred_element_type=F32)


def _dot_nt(a, b):
    return lax.dot_general(a, b, (((1,), (1,)), ((), ())), preferred_element_type=F32)


def _dot_tn(a, b):
    return lax.dot_general(a, b, (((0,), (0,)), ((), ())), preferred_element_type=F32)


def _ffn_kernel(x_ref, g_ref, wu_ref, wd_ref, o_ref, *, d_ff, n_chunks):
    x = x_ref[...]
    h = _rms(x, g_ref[...]).astype(BF16)
    ck = d_ff // n_chunks
    y = jnp.zeros_like(x)
    for c in range(n_chunks):
        a = _dot(h, wu_ref[:, c * ck:(c + 1) * ck])
        b = _dot(h, wu_ref[:, d_ff + c * ck:d_ff + (c + 1) * ck])
        act = (a * jax.nn.sigmoid(a) * b).astype(BF16)
        y = y + _dot(act, wd_ref[c * ck:(c + 1) * ck, :])
    o_ref[...] = x + 0.5 * y


def _ffn(x2, g, w_up, w_down, *, tm=1024):
    t, d = x2.shape
    d_ff = w_down.shape[0]
    return pl.pallas_call(
        functools.partial(_ffn_kernel, d_ff=d_ff, n_chunks=2),
        out_shape=jax.ShapeDtypeStruct((t, d), F32),
        grid=(t // tm,),
        in_specs=[pl.BlockSpec((tm, d), lambda i: (i, 0)),
                  _resident((1, d)), _resident(w_up.shape), _resident(w_down.shape)],
        out_specs=pl.BlockSpec((tm, d), lambda i: (i, 0)),
        compiler_params=_params("parallel"),
        name="ffn",
    )(x2, g, w_up, w_down)


def _headnorm_store(y, g128, o_ref, transposed=False):
    lane = lax.broadcasted_iota(jnp.int32, (1, LANES), 1)
    lo = lane < HEAD_DIM
    for c in range(y.shape[1] // LANES):
        blk = y[:, c * LANES:(c + 1) * LANES]
        sq = blk * blk
        s_lo = jnp.sum(jnp.where(lo, sq, 0.0), axis=-1, keepdims=True)
        s_hi = jnp.sum(jnp.where(lo, 0.0, sq), axis=-1, keepdims=True)
        ms = jnp.where(lo, s_lo, s_hi) * (1.0 / HEAD_DIM)
        normed = blk * lax.rsqrt(ms + EPS) * g128
        if transposed:
            o_ref[c] = normed.T.astype(o_ref.dtype)
        else:
            o_ref[:, c * LANES:(c + 1) * LANES] = normed.astype(o_ref.dtype)


def _inproj_kernel(x_ref, g_ref, w_ref, b_ref, aqg_ref, akg_ref, bqg_ref, bkg_ref, perm_ref,
                   aq_o, ak_o, av_o, bq_o, bk_o, bv_o, z_o, gate_o, *regrouped, splits, dils):
    h = _rms(x_ref[...], g_ref[...]).astype(BF16)

    def proj(n):
        lo, hi = splits[n], splits[n + 1]
        return _dot(h, w_ref[:, lo:hi]) + b_ref[:, lo:hi]

    _headnorm_store(proj(0), aqg_ref[...], aq_o, transposed=True)
    _headnorm_store(proj(1), akg_ref[...], ak_o)
    av_o[...] = proj(2).astype(av_o.dtype)
    _headnorm_store(proj(3), bqg_ref[...], bq_o)
    _headnorm_store(proj(4), bkg_ref[...], bk_o)
    bv_o[...] = proj(5).astype(bv_o.dtype)
    cu = proj(6)
    c_ch = cu.shape[1] // 2
    z_o[...] = (cu[:, :c_ch] * jax.nn.sigmoid(cu[:, c_ch:])).astype(z_o.dtype)
    gate_o[...] = jax.nn.sigmoid(proj(7)).astype(gate_o.dtype)

    for n, dil in enumerate(dils):
        perm = perm_ref[n]
        qc_o, kg_o, vg_o = regrouped[3 * n:3 * n + 3]
        qc_o[...] = _dot(perm, bq_o[...]).astype(qc_o.dtype)
        group = CHUNK // dil
        for src, dst in ((bk_o, kg_o), (bv_o, vg_o)):
            moved = _dot(perm, src[...]).astype(dst.dtype)
            for r in range(dil):
                dst[r] = moved[r * group:(r + 1) * group, :]


def _chunk_perm(dil):
    src = np.arange(CHUNK)
    dst = (src % dil) * (CHUNK // dil) + src // dil
    perm = np.zeros((CHUNK, CHUNK), np.float32)
    perm[dst, src] = 1.0
    return perm


def _inproj(x2, g, w_in, b_in, aqg, akg, bqg, bkg, sizes, bn, dils):
    t, d = x2.shape
    tm = CHUNK
    n_chunks = t // bn // CHUNK
    splits = tuple(int(v) for v in np.concatenate([[0], np.cumsum(sizes)]))
    a_qk, _, a_v, b_w, _, _, c2, gate_w = sizes
    widths = (a_qk, a_qk, a_v, b_w, b_w, b_w, c2 // 2, gate_w)
    dtypes = (BF16, BF16, BF16, BF16, BF16, BF16, F32, BF16)
    row = lambda i: (i, 0)
    out_shape = [jax.ShapeDtypeStruct((t, w), dt) for w, dt in zip(widths, dtypes)]
    out_specs = [pl.BlockSpec((tm, w), row) for w in widths]
    out_shape[0] = jax.ShapeDtypeStruct((bn, a_qk // LANES, LANES, t // bn), BF16)
    out_specs[0] = pl.BlockSpec((None, a_qk // LANES, LANES, tm),
                                lambda i: (i // n_chunks, 0, 0, i % n_chunks))
    for dil in dils:
        group = CHUNK // dil
        grouped = jax.ShapeDtypeStruct((bn, dil, n_chunks, group, b_w), BF16)
        grouped_spec = pl.BlockSpec((None, dil, None, group, b_w),
                                    lambda i: (i // n_chunks, 0, i % n_chunks, 0, 0))
        out_shape += [jax.ShapeDtypeStruct((t, b_w), BF16), grouped, grouped]
        out_specs += [pl.BlockSpec((tm, b_w), row), grouped_spec, grouped_spec]
    perms = jnp.asarray(np.stack([_chunk_perm(dil) for dil in dils]), BF16)
    return pl.pallas_call(
        functools.partial(_inproj_kernel, splits=splits, dils=dils),
        out_shape=out_shape,
        grid=(t // tm,),
        in_specs=[pl.BlockSpec((tm, d), row), _resident((1, d)), _resident(w_in.shape),
                  _resident(b_in.shape)] + [_resident((1, LANES))] * 4 + [_resident(perms.shape)],
        out_specs=out_specs,
        compiler_params=_params("parallel"),
        name="inproj",
    )(x2, g, w_in, b_in, aqg, akg, bqg, bkg, perms)


def _diffattn_kernel(slopes_ref, lam_ref, gsub_ref, qb_ref, kb_ref, q_ref, k_ref, v_ref, o_ref,
                     vt_ref, t_ref, s_ref, m_ref, acc_ref, qt_ref, top_ref, *, tq, seq, lam0, n_sub):
    hd = pl.program_id(1)
    n_tiles = seq // tq
    sig = slopes_ref[hd]

    @pl.when(pl.program_id(2) == 0)
    def _():
        for c in range(n_tiles):
            vt_ref[c, :LANES, :] = v_ref[c * tq:(c + 1) * tq, :].astype(F32).T.astype(BF16)
            vt_ref[c, LANES:, :] = jnp.ones((BF16_ROWS, tq), BF16)
        t = -jnp.abs(lax.broadcasted_iota(jnp.int32, (tq, tq), 0)
                     - lax.broadcasted_iota(jnp.int32, (tq, tq), 1)).astype(F32) * sig
        t_ref[:, :tq] = t
        t_ref[:, tq:] = t

    for sub in range(n_sub):
        _diffattn_query_tile(pl.program_id(2) * n_sub + sub, sig, lam_ref, gsub_ref, qb_ref, kb_ref,
                             q_ref.at[:, sub * tq:(sub + 1) * tq], k_ref,
                             o_ref.at[:, sub * tq:(sub + 1) * tq], vt_ref, t_ref, s_ref,
                             m_ref.at[sub], acc_ref.at[sub], qt_ref.at[sub], top_ref,
                             tq=tq, n_tiles=n_tiles, lam0=lam0)


def _diffattn_query_tile(qi, sig, lam_ref, gsub_ref, qb_ref, kb_ref, q_ref, k_ref, o_ref, vt_ref,
                         t_ref, s_ref, m_ref, acc_ref, qt_ref, top_ref, *, tq, n_tiles, lam0):
    zero = jnp.zeros((HEAD_DIM, tq), BF16)
    qt_ref[:HEAD_DIM, :tq] = q_ref[:HEAD_DIM, :]
    qt_ref[HEAD_DIM:LANES, :tq] = zero
    qt_ref[:HEAD_DIM, tq:] = zero
    qt_ref[HEAD_DIM:LANES, tq:] = q_ref[HEAD_DIM:, :]
    qt_ref[LANES:, :tq] = qb_ref[...]
    qt_ref[LANES:, tq:] = qb_ref[...]

    def key_tile(j):
        return k_ref[pl.ds(pl.multiple_of(j * tq, tq), tq), :]

    def tile_index(jj):
        return jnp.where(jj == 0, qi, jj - (jj <= qi).astype(jnp.int32))

    def put_scores(slot, st):
        s_ref[slot] = st
        top_ref[slot] = jnp.max(st, axis=0, keepdims=True)

    def scores(jj, slot):
        j = tile_index(jj)
        after = (j > qi).astype(jnp.int32)
        put_scores(slot, _dot(jnp.concatenate([key_tile(j), kb_ref[after]], axis=1), qt_ref[...]))

    def absorb(jj, slot):
        j = tile_index(jj)
        far = (jnp.abs(qi - j) * tq).astype(F32) * sig
        m = m_ref[...]
        m_new = jnp.maximum(m, top_ref[slot] - far)
        alpha = jnp.exp2(m - m_new)
        p = jnp.exp2(s_ref[slot] - (m_new + far)).astype(BF16)
        m_ref[...] = m_new
        acc_ref[...] = alpha * acc_ref[...] + _dot(vt_ref[j], p)

    m_ref[...] = jnp.full(m_ref.shape, NEG_BIG, F32)
    acc_ref[...] = jnp.zeros(acc_ref.shape, F32)
    put_scores(0, _dot(key_tile(qi), qt_ref[:LANES, :]) + t_ref[...])

    scores(1, 1)
    absorb(0, 0)

    def triple(i, carry):
        for n in range(1, 4):
            scores(3 * i + n + 1, (n + 1) % 3)
            absorb(3 * i + n, n % 3)
        return carry

    lax.fori_loop(0, (n_tiles - 2) // 3, triple, 0)
    absorb(n_tiles - 1, (n_tiles - 1) % 3)
    acc = acc_ref[...]

    lp = lam_ref[...]
    d1 = jnp.sum(lp[0:1] * lp[1:2], axis=-1, keepdims=True)
    d2 = jnp.sum(lp[2:3] * lp[3:4], axis=-1, keepdims=True)
    lam = jnp.exp(d1) - jnp.exp(d2) + lam0
    yt = acc[:LANES] / acc[LANES:LANES + 1]
    yt = yt[:, :tq] - lam * yt[:, tq:]
    ms = jnp.mean(yt * yt, axis=0, keepdims=True)
    o_ref[...] = (yt * lax.rsqrt(ms + EPS) * (gsub_ref[...] * (1.0 - lam0))).astype(o_ref.dtype)


def _split3(x):
    p1 = x.astype(BF16)
    r1 = x - p1.astype(F32)
    p2 = r1.astype(BF16)
    p3 = (r1 - p2.astype(F32)).astype(BF16)
    return [p1, p2, p3]


def _alibi_lanes(slopes, tq):
    pos = (slopes[:, None] * np.arange(tq, dtype=np.float32)[None, :]).astype(np.float32)
    one = np.ones_like(pos)
    pad = np.zeros(pos.shape + (LANES - 6,), np.float32)
    stack = lambda cols: np.concatenate([np.stack(cols, axis=-1), pad], axis=-1)
    pieces = [np.asarray(p, np.float32) for p in _split3(pos)]
    q_side = stack([one, one, one] + pieces)
    k_side = stack(pieces + [-one, -one, -one])
    return (jnp.asarray(q_side.transpose(0, 2, 1), BF16),
            jnp.asarray(np.stack([k_side, -k_side], axis=1), BF16))


def _diffattn(aqt, ak, av, slopes, lam_p, gsub, *, lam0, tq=512, n_sub=4):
    bn, seq, _ = ak.shape
    smem = pl.BlockSpec(memory_space=pltpu.SMEM)
    qb, kb = _alibi_lanes(slopes, tq)
    return pl.pallas_call(
        functools.partial(_diffattn_kernel, tq=tq, seq=seq, lam0=lam0, n_sub=n_sub),
        out_shape=jax.ShapeDtypeStruct(aqt.shape, BF16),
        grid=(bn, A_HEADS, seq // (tq * n_sub)),
        in_specs=[smem, _resident(lam_p.shape), _resident((LANES, 1)),
                  pl.BlockSpec((None, LANES, tq), lambda b, h, i: (h, 0, 0)),
                  pl.BlockSpec((None, 2, tq, LANES), lambda b, h, i: (h, 0, 0, 0)),
                  pl.BlockSpec((None, None, LANES, tq * n_sub), lambda b, h, i: (b, h, 0, i)),
                  pl.BlockSpec((None, seq, LANES), lambda b, h, i: (b, 0, h)),
                  pl.BlockSpec((None, seq, LANES), lambda b, h, i: (b, 0, h))],
        out_specs=pl.BlockSpec((None, None, LANES, tq * n_sub), lambda b, h, i: (b, h, 0, i)),
        scratch_shapes=[pltpu.VMEM((seq // tq, LANES + BF16_ROWS, tq), BF16),
                        pltpu.VMEM((tq, 2 * tq), F32),
                        pltpu.VMEM((3, tq, 2 * tq), F32),
                        pltpu.VMEM((n_sub, 1, 2 * tq), F32),
                        pltpu.VMEM((n_sub, LANES + BF16_ROWS, 2 * tq), F32),
                        pltpu.VMEM((n_sub, 2 * LANES, 2 * tq), BF16),
                        pltpu.VMEM((3, 1, 2 * tq), F32)],
        compiler_params=_params("parallel", "parallel", "arbitrary"),
        name="diffattn",
    )(jnp.asarray(slopes), lam_p, gsub, qb, kb, aqt, ak, av)


def _dilated_kernel(bias_ref, inv_ref, q_ref, k_ref, v_ref, o_ref, st_ref, n_ref, rows_ref, *,
                    dil, u_len, n_sub, per_iter):
    b_w = o_ref.shape[1]
    n_pairs = b_w // LANES
    win = DIL_TU + 2 * DIL_RADIUS
    group = CHUNK // dil if dil > 1 else DIL_TU
    n_slabs = DIL_TU // group
    lane = lax.broadcasted_iota(jnp.int32, (1, LANES), 1)
    lo = lane < HEAD_DIM
    rows_ref[...] = jnp.zeros(rows_ref.shape, F32)

    def tile(t, rows_ref):
        r = t % dil
        sub = t // dil
        u0 = (pl.program_id(1) * n_sub + sub) * DIL_TU
        start = pl.multiple_of(jnp.clip(u0 - DIL_RADIUS, 0, u_len - win), DIL_RADIUS)
        variant = (u0 - start) // DIL_RADIUS
        base = pl.multiple_of(r * u_len + start, DIL_RADIUS)

        def slab(c):
            row = (sub * n_slabs + c) * (CHUNK if dil > 1 else DIL_TU) + r * group
            return pl.ds(pl.multiple_of(row, group), group)

        for pair in range(n_pairs):
            cols = slice(pair * LANES, (pair + 1) * LANES)
            q = jnp.concatenate([q_ref[slab(c), cols] for c in range(n_slabs)], axis=0)
            zero = jnp.zeros_like(q)
            qcat = jnp.concatenate([jnp.where(lo, q, zero), jnp.where(lo, zero, q)], axis=0)
            st = _dot_nt(k_ref[pl.ds(base, win), cols], qcat) + bias_ref[pair, variant]
            m = jnp.max(st, axis=0, keepdims=True)
            p = jnp.exp2(st - m)
            rows_ref[2 * pair:2 * pair + 1, :] = m[:, :DIL_TU]
            rows_ref[2 * pair + 1:2 * pair + 2, :] = m[:, DIL_TU:]
            den = jnp.sum(p, axis=0, keepdims=True)
            rows_ref[DEN_ROW + 2 * pair:DEN_ROW + 2 * pair + 1, :] = den[:, :DIL_TU]
            rows_ref[DEN_ROW + 2 * pair + 1:DEN_ROW + 2 * pair + 2, :] = den[:, DIL_TU:]
            pb = p.astype(BF16)
            vw = v_ref[pl.ds(base, win), cols]
            o0 = _dot_tn(pb[:, :DIL_TU], vw)
            o1 = _dot_tn(pb[:, DIL_TU:], vw)
            out = jnp.where(lo, o0, o1)
            for c in range(n_slabs):
                n_ref[slab(c), cols] = out[c * group:(c + 1) * group, :].astype(n_ref.dtype)
        stats = rows_ref[...].T
        pieces = [stats] if dil == 1 else _split3(stats)
        for n, piece in enumerate(pieces):
            cols = slice(b_w + n * LANES, b_w + (n + 1) * LANES)
            for c in range(n_slabs):
                n_ref[slab(c), cols] = piece[c * group:(c + 1) * group, :].astype(n_ref.dtype)

    def tiles(it, carry):
        for n in range(per_iter):
            tile(it * per_iter + n, rows_ref.at[n])
        return carry

    lax.fori_loop(0, n_sub * dil // per_iter, tiles, 0)

    if dil == 1:
        o_ref[...] = n_ref[:, :b_w].astype(o_ref.dtype)
        st_ref[...] = n_ref[:, b_w:]
    else:
        for c in range(o_ref.shape[0] // CHUNK):
            rows = slice(c * CHUNK, (c + 1) * CHUNK)
            nat = _dot(inv_ref[...], n_ref[rows, :])
            o_ref[rows, :] = nat[:, :b_w].astype(o_ref.dtype)
            st_ref[rows, :] = (nat[:, b_w:b_w + LANES] + nat[:, b_w + LANES:b_w + 2 * LANES]
                               + nat[:, b_w + 2 * LANES:])


def _dilated_bias(slopes, dil):
    win = DIL_TU + 2 * DIL_RADIUS
    r = np.arange(win)[:, None]
    c = np.arange(DIL_TU)[None, :]
    du = np.stack([np.abs(c + off - r) for off in (0, DIL_RADIUS, 2 * DIL_RADIUS)])
    alibi = -(du * dil).astype(np.float32)[None] * slopes[:, None, None, None]
    table = np.where(du[None] <= DIL_RADIUS, alibi, np.float32(NEG_BIG))
    table = table.reshape(len(slopes) // 2, 2, 3, win, DIL_TU).transpose(0, 2, 3, 1, 4)
    return jnp.asarray(table.reshape(len(slopes) // 2, 3, win, 2 * DIL_TU), F32)


def _dilated(qc, kg, vg, slopes, dil):
    bn, seq, b_w = qc.shape
    u_len = seq // dil
    n_sub = max(1, DIL_TOKENS // (DIL_TU * dil))
    tl = n_sub * DIL_TU * dil
    per_iter = 16
    extra = LANES if dil == 1 else 3 * LANES
    n_dtype = F32 if dil == 1 else BF16
    bias = _dilated_bias(slopes, dil)
    inv = jnp.asarray(_chunk_perm(dil).T, BF16)
    o, st = pl.pallas_call(
        functools.partial(_dilated_kernel, dil=dil, u_len=u_len, n_sub=n_sub, per_iter=per_iter),
        out_shape=[jax.ShapeDtypeStruct((bn, seq, b_w), BF16),
                   jax.ShapeDtypeStruct((bn, seq, LANES), F32)],
        grid=(bn, seq // tl),
        in_specs=[_resident(bias.shape), _resident(inv.shape),
                  pl.BlockSpec((None, tl, b_w), lambda b, i: (b, i, 0)),
                  pl.BlockSpec((None, seq, b_w), lambda b, i: (b, 0, 0), pipeline_mode=pl.Buffered(1)),
                  pl.BlockSpec((None, seq, b_w), lambda b, i: (b, 0, 0), pipeline_mode=pl.Buffered(1))],
        out_specs=[pl.BlockSpec((None, tl, b_w), lambda b, i: (b, i, 0)),
                   pl.BlockSpec((None, tl, LANES), lambda b, i: (b, i, 0))],
        scratch_shapes=[pltpu.VMEM((tl, b_w + extra), n_dtype),
                        pltpu.VMEM((per_iter, LANES, DIL_TU), F32)],
        compiler_params=_params("parallel", "parallel"),
        name=f"dilated{dil}",
    )(bias, inv, qc, kg, vg)
    return o.reshape(bn * seq, b_w), st.reshape(bn * seq, LANES)


def _conv_kernel(prev_ref, cur_ref, next_ref, w_ref, b_ref, g_ref, o_ref, pad_ref, cv_ref, *, ts, tr):
    i = pl.program_id(1)
    n = pl.num_programs(1)
    prev = prev_ref[...]
    nxt = next_ref[...]
    pad_ref[0:CONV_HALO, :] = jnp.where(i > 0, prev, jnp.zeros_like(prev))
    pad_ref[CONV_HALO:CONV_HALO + ts, :] = cur_ref[...]
    pad_ref[CONV_HALO + ts:, :] = jnp.where(i < n - 1, nxt, jnp.zeros_like(nxt))
    base = CONV_HALO - CONV_W // 2
    wrows = tr + 2 * CONV_HALO

    def rows(r, carry):
        t0 = pl.multiple_of(r * tr, tr)
        for c in range(o_ref.shape[1] // LANES):
            cols = slice(c * LANES, (c + 1) * LANES)
            win = pad_ref[pl.ds(t0, wrows), cols]
            acc = jnp.zeros((tr, LANES), F32) + b_ref[:, cols]
            for b in range(SUBLANES):
                wb = win if b == 0 else pltpu.roll(win, shift=wrows - b, axis=0)
                for a in range(2 * CONV_HALO // SUBLANES):
                    tap = SUBLANES * a + b - base
                    if 0 <= tap < CONV_W:
                        acc = acc + wb[SUBLANES * a:SUBLANES * a + tr, :] * w_ref[tap:tap + 1, cols]
            cv_ref[pl.ds(t0, tr), cols] = acc
        y = _rms(cv_ref[pl.ds(t0, tr), :], g_ref[...])
        o_ref[pl.ds(t0, tr), :] = (y * jax.nn.sigmoid(y)).astype(o_ref.dtype)
        return carry

    lax.fori_loop(0, ts // tr, rows, 0)


def _conv(z, dw_w, dw_b, norm_g, *, ts=1024, tr=128):
    bn, seq, c = z.shape
    per = ts // CONV_HALO
    last = seq // CONV_HALO - 1
    return pl.pallas_call(
        functools.partial(_conv_kernel, ts=ts, tr=tr),
        out_shape=jax.ShapeDtypeStruct((bn, seq, c), BF16),
        grid=(bn, seq // ts),
        in_specs=[pl.BlockSpec((None, CONV_HALO, c), lambda b, i: (b, jnp.maximum(i * per - 1, 0), 0)),
                  pl.BlockSpec((None, ts, c), lambda b, i: (b, i, 0)),
                  pl.BlockSpec((None, CONV_HALO, c), lambda b, i: (b, jnp.minimum((i + 1) * per, last), 0)),
                  _resident(dw_w.shape), _resident((1, c)), _resident((1, c))],
        out_specs=pl.BlockSpec((None, ts, c), lambda b, i: (b, i, 0)),
        scratch_shapes=[pltpu.VMEM((ts + 2 * CONV_HALO, c), F32), pltpu.VMEM((ts, c), F32)],
        compiler_params=_params("parallel", "parallel"),
        name="conv",
    )(z, z, z, dw_w, dw_b, norm_g)


def _merge_kernel(x_ref, ya_ref, o1_ref, o2_ref, o3_ref, l1_ref, l2_ref, l3_ref, yc_ref, gate_ref,
                  exp_ref, wa_ref, wb_ref, wc_ref, wo_ref, out_ref, *, d):
    stats = (l1_ref[...], l2_ref[...], l3_ref[...])
    dens = [pltpu.roll(s, LANES - DEN_ROW, axis=1) for s in stats]
    top = jnp.maximum(jnp.maximum(stats[0], stats[1]), stats[2])
    es = [jnp.exp2(s - top) for s in stats]
    total = es[0] * dens[0] + es[1] * dens[1] + es[2] * dens[2]
    heads = lax.broadcasted_iota(jnp.int32, (1, LANES), 1) < exp_ref.shape[1] // HEAD_DIM
    inv = jnp.where(heads, 1.0 / total, 0.0)
    mix = None
    for e, o_ref in zip(es, (o1_ref, o2_ref, o3_ref)):
        w = e * inv
        wide = _dot(w.astype(BF16), exp_ref[...])
        term = wide * o_ref[...].astype(F32)
        mix = term if mix is None else mix + term
    slab = ya_ref.shape[1]
    ya = _dot_tn(ya_ref[0], wa_ref[0:slab, :])
    for n in range(1, ya_ref.shape[0]):
        ya = ya + _dot_tn(ya_ref[n], wa_ref[n * slab:(n + 1) * slab, :])
    yb = _dot(mix.astype(BF16), wb_ref[...])
    yc = _dot(yc_ref[...], wc_ref[...])
    merged = (gate_ref[:, 0:d].astype(F32) * ya + gate_ref[:, d:2 * d].astype(F32) * yb
              + gate_ref[:, 2 * d:3 * d].astype(F32) * yc)
    out_ref[...] = x_ref[...] + _dot(merged.astype(BF16), wo_ref[...])


def _merge(x2, yat, outs, lses, yc, gates, expand, wa, wb, wc, wo, *, tm=512):
    t, d = x2.shape
    row = lambda i: (i, 0)
    per_batch = yat.shape[3] // tm
    tiles = [x2, *outs, *lses, yc, gates]
    weights = [expand, wa, wb, wc, wo]
    specs = [pl.BlockSpec((tm, a.shape[1]), row) for a in tiles]
    specs.insert(1, pl.BlockSpec((None,) + yat.shape[1:3] + (tm,),
                                 lambda i: (i // per_batch, 0, 0, i % per_batch)))
    tiles.insert(1, yat)
    return pl.pallas_call(
        functools.partial(_merge_kernel, d=d),
        out_shape=jax.ShapeDtypeStruct((t, d), F32),
        grid=(t // tm,),
        in_specs=specs + [_resident(w.shape) for w in weights],
        out_specs=pl.BlockSpec((tm, d), row),
        compiler_params=_params("parallel"),
        name="merge",
    )(*tiles, *weights)


def _alibi_slopes(n):
    return (2.0 ** (-8.0 * np.arange(1, n + 1) / n) * LOG2E).astype(np.float32)


def _pair_gain(g, scale=1.0):
    return (jnp.concatenate([g, g]) * scale).reshape(1, LANES).astype(F32)


def kernel(x, ffn1_norm, ffn1_w_up, ffn1_w_down, mix_norm, w_in, b_in, a_q_norm, a_k_norm, a_lambda, a_sub_norm, w_out_a, b_q_norm, b_k_norm, w_out_b, c_dw_w, c_dw_b, c_norm, w_out_c, w_out, ffn2_norm, ffn2_w_up, ffn2_w_down):
    bn, seq, d = x.shape
    depth = w_in.shape[0]
    a_w = w_out_a.shape[1]
    b_w = w_out_b.shape[1]
    c_ch = w_out_c.shape[1]
    sizes = (a_w, a_w, a_w, b_w, b_w, b_w, 2 * c_ch, 3 * d)
    slopes_a = _alibi_slopes(A_HEADS)
    slopes_b = _alibi_slopes(B_HEADS)
    expand = (jnp.arange(LANES)[:, None] == (jnp.arange(b_w)[None, :] // HEAD_DIM)).astype(BF16)
    row = lambda v: v.reshape(1, -1).astype(F32)

    x2 = x.reshape(bn * seq, d)
    for l in range(depth):
        x2 = _ffn(x2, row(ffn1_norm[l]), ffn1_w_up[l].astype(BF16), ffn1_w_down[l].astype(BF16))

        dils = tuple(dil for _, dil in DILATED_PATTERNS if dil > 1)
        aqt, ak, av, bq, bk, bv, z, gates, *regrouped = _inproj(
            x2, row(mix_norm[l]), w_in[l].astype(BF16), row(b_in[l]),
            _pair_gain(a_q_norm[l], ATTN_SCALE * LOG2E), _pair_gain(a_k_norm[l]),
            _pair_gain(b_q_norm[l], ATTN_SCALE * LOG2E), _pair_gain(b_k_norm[l]), sizes, bn, dils)

        lam0 = 0.8 - 0.6 * math.exp(-0.3 * l)
        to3 = lambda t: t.reshape(bn, seq, t.shape[-1])
        yat = _diffattn(aqt, to3(ak), to3(av), slopes_a, a_lambda[l].astype(F32),
                        a_sub_norm[l].reshape(-1, 1).astype(F32), lam0=lam0)
        yat = yat.reshape(bn, A_HEADS // 2, 2 * LANES, seq)

        outs, lses = [], []
        for _, dil in DILATED_PATTERNS:
            qkv = (bq, bk, bv) if dil == 1 else regrouped[3 * dils.index(dil):3 * dils.index(dil) + 3]
            o, lse = _dilated(*(to3(t) for t in qkv), slopes_b, dil)
            outs.append(o)
            lses.append(lse)

        yc = _conv(to3(z), c_dw_w[l].astype(F32), row(c_dw_b[l]), row(c_norm[l])).reshape(bn * seq, c_ch)

        x2 = _merge(x2, yat, outs, lses, yc, gates, expand,
                    w_out_a[l].astype(BF16), w_out_b[l].astype(BF16), w_out_c[l].astype(BF16),
                    w_out[l].astype(BF16))

        x2 = _ffn(x2, row(ffn2_norm[l]), ffn2_w_up[l].astype(BF16), ffn2_w_down[l].astype(BF16))
    return x2.reshape(bn, seq, d)
```

```python
import functools
import math

import jax
import jax.numpy as jnp
import numpy as np
from jax import lax
from jax.experimental import pallas as pl
from jax.experimental.pallas import tpu as pltpu

HEAD_DIM = 64
LANES = 128
SUBLANES = 8
BF16_ROWS = 16
A_HEADS = 8
B_HEADS = 12
CONV_W = 31
CONV_HALO = 16
DIL_RADIUS = 64
DIL_TU = 128
DIL_TOKENS = 2048
CHUNK = 256
DEN_ROW = 16
DILATED_PATTERNS = ((128, 1), (512, 4), (2048, 16))
ATTN_SCALE = HEAD_DIM ** -0.5
LOG2E = math.log2(math.e)
EPS = 1e-6
NEG_BIG = -1e30
VMEM_LIMIT = 56 * 1024 * 1024

BF16 = jnp.bfloat16
F32 = jnp.float32


def _params(*sem):
    return pltpu.CompilerParams(dimension_semantics=sem, vmem_limit_bytes=VMEM_LIMIT)


def _resident(shape):
    nd = len(shape)
    return pl.BlockSpec(shape, lambda *_: (0,) * nd, pipeline_mode=pl.Buffered(1))


def _rms(xf, g):
    ms = jnp.mean(xf * xf, axis=-1, keepdims=True)
    return xf * lax.rsqrt(ms + EPS) * g


def _dot(a, b):
    return jnp.dot(a, b, preferred_element_type=F32)


def _dot_nt(a, b):
    return lax.dot_general(a, b, (((1,), (1,)), ((), ())), preferred_element_type=F32)


def _dot_tn(a, b):
    return lax.dot_general(a, b, (((0,), (0,)), ((), ())), preferred_element_type=F32)


def _ffn_kernel(x_ref, g_ref, wu_ref, wd_ref, o_ref, *, d_ff, n_chunks):
    x = x_ref[...]
    h = _rms(x, g_ref[...]).astype(BF16)
    ck = d_ff // n_chunks
    y = jnp.zeros_like(x)
    for c in range(n_chunks):
        a = _dot(h, wu_ref[:, c * ck:(c + 1) * ck])
        b = _dot(h, wu_ref[:, d_ff + c * ck:d_ff + (c + 1) * ck])
        act = (a * jax.nn.sigmoid(a) * b).astype(BF16)
        y = y + _dot(act, wd_ref[c * ck:(c + 1) * ck, :])
    o_ref[...] = x + 0.5 * y


def _ffn(x2, g, w_up, w_down, *, tm=1024):
    t, d = x2.shape
    d_ff = w_down.shape[0]
    return pl.pallas_call(
        functools.partial(_ffn_kernel, d_ff=d_ff, n_chunks=2),
        out_shape=jax.ShapeDtypeStruct((t, d), F32),
        grid=(t // tm,),
        in_specs=[pl.BlockSpec((tm, d), lambda i: (i, 0)),
                  _resident((1, d)), _resident(w_up.shape), _resident(w_down.shape)],
        out_specs=pl.BlockSpec((tm, d), lambda i: (i, 0)),
        compiler_params=_params("parallel"),
        name="ffn",
    )(x2, g, w_up, w_down)


def _headnorm_store(y, g128, o_ref, transposed=False):
    lane = lax.broadcasted_iota(jnp.int32, (1, LANES), 1)
    lo = lane < HEAD_DIM
    for c in range(y.shape[1] // LANES):
        blk = y[:, c * LANES:(c + 1) * LANES]
        sq = blk * blk
        s_lo = jnp.sum(jnp.where(lo, sq, 0.0), axis=-1, keepdims=True)
        s_hi = jnp.sum(jnp.where(lo, 0.0, sq), axis=-1, keepdims=True)
        ms = jnp.where(lo, s_lo, s_hi) * (1.0 / HEAD_DIM)
        normed = blk * lax.rsqrt(ms + EPS) * g128
        if transposed:
            o_ref[c] = normed.T.astype(o_ref.dtype)
        else:
            o_ref[:, c * LANES:(c + 1) * LANES] = normed.astype(o_ref.dtype)


def _inproj_kernel(x_ref, g_ref, w_ref, b_ref, aqg_ref, akg_ref, bqg_ref, bkg_ref, perm_ref,
                   aq_o, ak_o, av_o, bq_o, bk_o, bv_o, z_o, gate_o, *regrouped, splits, dils):
    h = _rms(x_ref[...], g_ref[...]).astype(BF16)

    def proj(n):
        lo, hi = splits[n], splits[n + 1]
        return _dot(h, w_ref[:, lo:hi]) + b_ref[:, lo:hi]

    _headnorm_store(proj(0), aqg_ref[...], aq_o, transposed=True)
    _headnorm_store(proj(1), akg_ref[...], ak_o)
    av_o[...] = proj(2).astype(av_o.dtype)
    _headnorm_store(proj(3), bqg_ref[...], bq_o)
    _headnorm_store(proj(4), bkg_ref[...], bk_o)
    bv_o[...] = proj(5).astype(bv_o.dtype)
    cu = proj(6)
    c_ch = cu.shape[1] // 2
    z_o[...] = (cu[:, :c_ch] * jax.nn.sigmoid(cu[:, c_ch:])).astype(z_o.dtype)
    gate_o[...] = jax.nn.sigmoid(proj(7)).astype(gate_o.dtype)

    for n, dil in enumerate(dils):
        perm = perm_ref[n]
        qc_o, kg_o, vg_o = regrouped[3 * n:3 * n + 3]
        qc_o[...] = _dot(perm, bq_o[...]).astype(qc_o.dtype)
        group = CHUNK // dil
        for src, dst in ((bk_o, kg_o), (bv_o, vg_o)):
            moved = _dot(perm, src[...]).astype(dst.dtype)
            for r in range(dil):
                dst[r] = moved[r * group:(r + 1) * group, :]


def _chunk_perm(dil):
    src = np.arange(CHUNK)
    dst = (src % dil) * (CHUNK // dil) + src // dil
    perm = np.zeros((CHUNK, CHUNK), np.float32)
    perm[dst, src] = 1.0
    return perm


def _inproj(x2, g, w_in, b_in, aqg, akg, bqg, bkg, sizes, bn, dils):
    t, d = x2.shape
    tm = CHUNK
    n_chunks = t // bn // CHUNK
    splits = tuple(int(v) for v in np.concatenate([[0], np.cumsum(sizes)]))
    a_qk, _, a_v, b_w, _, _, c2, gate_w = sizes
    widths = (a_qk, a_qk, a_v, b_w, b_w, b_w, c2 // 2, gate_w)
    dtypes = (BF16, BF16, BF16, BF16, BF16, BF16, F32, BF16)
    row = lambda i: (i, 0)
    out_shape = [jax.ShapeDtypeStruct((t, w), dt) for w, dt in zip(widths, dtypes)]
    out_specs = [pl.BlockSpec((tm, w), row) for w in widths]
    out_shape[0] = jax.ShapeDtypeStruct((bn, a_qk // LANES, LANES, t // bn), BF16)
    out_specs[0] = pl.BlockSpec((None, a_qk // LANES, LANES, tm),
                                lambda i: (i // n_chunks, 0, 0, i % n_chunks))
    for dil in dils:
        group = CHUNK // dil
        grouped = jax.ShapeDtypeStruct((bn, dil, n_chunks, group, b_w), BF16)
        grouped_spec = pl.BlockSpec((None, dil, None, group, b_w),
                                    lambda i: (i // n_chunks, 0, i % n_chunks, 0, 0))
        out_shape += [jax.ShapeDtypeStruct((t, b_w), BF16), grouped, grouped]
        out_specs += [pl.BlockSpec((tm, b_w), row), grouped_spec, grouped_spec]
    perms = jnp.asarray(np.stack([_chunk_perm(dil) for dil in dils]), BF16)
    return pl.pallas_call(
        functools.partial(_inproj_kernel, splits=splits, dils=dils),
        out_shape=out_shape,
        grid=(t // tm,),
        in_specs=[pl.BlockSpec((tm, d), row), _resident((1, d)), _resident(w_in.shape),
                  _resident(b_in.shape)] + [_resident((1, LANES))] * 4 + [_resident(perms.shape)],
        out_specs=out_specs,
        compiler_params=_params("parallel"),
        name="inproj",
    )(x2, g, w_in, b_in, aqg, akg, bqg, bkg, perms)


def _diffattn_kernel(slopes_ref, lam_ref, gsub_ref, qb_ref, kb_ref, q_ref, k_ref, v_ref, o_ref,
                     vt_ref, t_ref, s_ref, m_ref, acc_ref, qt_ref, top_ref, *, tq, seq, lam0, n_sub):
    hd = pl.program_id(1)
    n_tiles = seq // tq
    sig = slopes_ref[hd]

    @pl.when(pl.program_id(2) == 0)
    def _():
        for c in range(n_tiles):
            vt_ref[c, :LANES, :] = v_ref[c * tq:(c + 1) * tq, :].astype(F32).T.astype(BF16)
            vt_ref[c, LANES:, :] = jnp.ones((BF16_ROWS, tq), BF16)
        t = -jnp.abs(lax.broadcasted_iota(jnp.int32, (tq, tq), 0)
                     - lax.broadcasted_iota(jnp.int32, (tq, tq), 1)).astype(F32) * sig
        t_ref[:, :tq] = t
        t_ref[:, tq:] = t

    for sub in range(n_sub):
        _diffattn_query_tile(pl.program_id(2) * n_sub + sub, sig, lam_ref, gsub_ref, qb_ref, kb_ref,
                             q_ref.at[:, sub * tq:(sub + 1) * tq], k_ref,
                             o_ref.at[:, sub * tq:(sub + 1) * tq], vt_ref, t_ref, s_ref,
                             m_ref.at[sub], acc_ref.at[sub], qt_ref.at[sub], top_ref,
                             tq=tq, n_tiles=n_tiles, lam0=lam0)


def _diffattn_query_tile(qi, sig, lam_ref, gsub_ref, qb_ref, kb_ref, q_ref, k_ref, o_ref, vt_ref,
                         t_ref, s_ref, m_ref, acc_ref, qt_ref, top_ref, *, tq, n_tiles, lam0):
    zero = jnp.zeros((HEAD_DIM, tq), BF16)
    qt_ref[:HEAD_DIM, :tq] = q_ref[:HEAD_DIM, :]
    qt_ref[HEAD_DIM:LANES, :tq] = zero
    qt_ref[:HEAD_DIM, tq:] = zero
    qt_ref[HEAD_DIM:LANES, tq:] = q_ref[HEAD_DIM:, :]
    qt_ref[LANES:, :tq] = qb_ref[...]
    qt_ref[LANES:, tq:] = qb_ref[...]

    def key_tile(j):
        return k_ref[pl.ds(pl.multiple_of(j * tq, tq), tq), :]

    def tile_index(jj):
        return jnp.where(jj == 0, qi, jj - (jj <= qi).astype(jnp.int32))

    def put_scores(slot, st):
        s_ref[slot] = st
        top_ref[slot] = jnp.max(st, axis=0, keepdims=True)

    def scores(jj, slot):
        j = tile_index(jj)
        after = (j > qi).astype(jnp.int32)
        put_scores(slot, _dot(jnp.concatenate([key_tile(j), kb_ref[after]], axis=1), qt_ref[...]))

    def absorb(jj, slot):
        j = tile_index(jj)
        far = (jnp.abs(qi - j) * tq).astype(F32) * sig
        m = m_ref[...]
        m_new = jnp.maximum(m, top_ref[slot] - far)
        alpha = jnp.exp2(m - m_new)
        p = jnp.exp2(s_ref[slot] - (m_new + far)).astype(BF16)
        m_ref[...] = m_new
        acc_ref[...] = alpha * acc_ref[...] + _dot(vt_ref[j], p)

    m_ref[...] = jnp.full(m_ref.shape, NEG_BIG, F32)
    acc_ref[...] = jnp.zeros(acc_ref.shape, F32)
    put_scores(0, _dot(key_tile(qi), qt_ref[:LANES, :]) + t_ref[...])

    scores(1, 1)
    absorb(0, 0)

    def triple(i, carry):
        for n in range(1, 4):
            scores(3 * i + n + 1, (n + 1) % 3)
            absorb(3 * i + n, n % 3)
        return carry

    lax.fori_loop(0, (n_tiles - 2) // 3, triple, 0)
    absorb(n_tiles - 1, (n_tiles - 1) % 3)
    acc = acc_ref[...]

    lp = lam_ref[...]
    d1 = jnp.sum(lp[0:1] * lp[1:2], axis=-1, keepdims=True)
    d2 = jnp.sum(lp[2:3] * lp[3:4], axis=-1, keepdims=True)
    lam = jnp.exp(d1) - jnp.exp(d2) + lam0
    yt = acc[:LANES] / acc[LANES:LANES + 1]
    yt = yt[:, :tq] - lam * yt[:, tq:]
    ms = jnp.mean(yt * yt, axis=0, keepdims=True)
    o_ref[...] = (yt * lax.rsqrt(ms + EPS) * (gsub_ref[...] * (1.0 - lam0))).astype(o_ref.dtype)


def _split3(x):
    p1 = x.astype(BF16)
    r1 = x - p1.astype(F32)
    p2 = r1.astype(BF16)
    p3 = (r1 - p2.astype(F32)).astype(BF16)
    return [p1, p2, p3]


def _alibi_lanes(slopes, tq):
    pos = (slopes[:, None] * np.arange(tq, dtype=np.float32)[None, :]).astype(np.float32)
    one = np.ones_like(pos)
    pad = np.zeros(pos.shape + (LANES - 6,), np.float32)
    stack = lambda cols: np.concatenate([np.stack(cols, axis=-1), pad], axis=-1)
    pieces = [np.asarray(p, np.float32) for p in _split3(pos)]
    q_side = stack([one, one, one] + pieces)
    k_side = stack(pieces + [-one, -one, -one])
    return (jnp.asarray(q_side.transpose(0, 2, 1), BF16),
            jnp.asarray(np.stack([k_side, -k_side], axis=1), BF16))


def _diffattn(aqt, ak, av, slopes, lam_p, gsub, *, lam0, tq=512, n_sub=8):
    bn, seq, _ = ak.shape
    smem = pl.BlockSpec(memory_space=pltpu.SMEM)
    qb, kb = _alibi_lanes(slopes, tq)
    return pl.pallas_call(
        functools.partial(_diffattn_kernel, tq=tq, seq=seq, lam0=lam0, n_sub=n_sub),
        out_shape=jax.ShapeDtypeStruct(aqt.shape, BF16),
        grid=(bn, A_HEADS, seq // (tq * n_sub)),
        in_specs=[smem, _resident(lam_p.shape), _resident((LANES, 1)),
                  pl.BlockSpec((None, LANES, tq), lambda b, h, i: (h, 0, 0)),
                  pl.BlockSpec((None, 2, tq, LANES), lambda b, h, i: (h, 0, 0, 0)),
                  pl.BlockSpec((None, None, LANES, tq * n_sub), lambda b, h, i: (b, h, 0, i)),
                  pl.BlockSpec((None, seq, LANES), lambda b, h, i: (b, 0, h)),
                  pl.BlockSpec((None, seq, LANES), lambda b, h, i: (b, 0, h))],
        out_specs=pl.BlockSpec((None, None, LANES, tq * n_sub), lambda b, h, i: (b, h, 0, i)),
        scratch_shapes=[pltpu.VMEM((seq // tq, LANES + BF16_ROWS, tq), BF16),
                        pltpu.VMEM((tq, 2 * tq), F32),
                        pltpu.VMEM((3, tq, 2 * tq), F32),
                        pltpu.VMEM((n_sub, 1, 2 * tq), F32),
                        pltpu.VMEM((n_sub, LANES + BF16_ROWS, 2 * tq), F32),
                        pltpu.VMEM((n_sub, 2 * LANES, 2 * tq), BF16),
                        pltpu.VMEM((3, 1, 2 * tq), F32)],
        compiler_params=_params("parallel", "parallel", "arbitrary"),
        name="diffattn",
    )(jnp.asarray(slopes), lam_p, gsub, qb, kb, aqt, ak, av)


def _dilated_kernel(bias_ref, inv_ref, q_ref, k_ref, v_ref, o_ref, st_ref, n_ref, rows_ref, *,
                    dil, u_len, n_sub, per_iter):
    b_w = o_ref.shape[1]
    n_pairs = b_w // LANES
    win = DIL_TU + 2 * DIL_RADIUS
    group = CHUNK // dil if dil > 1 else DIL_TU
    n_slabs = DIL_TU // group
    lane = lax.broadcasted_iota(jnp.int32, (1, LANES), 1)
    lo = lane < HEAD_DIM
    rows_ref[...] = jnp.zeros(rows_ref.shape, F32)

    def tile(t, rows_ref):
        r = t % dil
        sub = t // dil
        u0 = (pl.program_id(1) * n_sub + sub) * DIL_TU
        start = pl.multiple_of(jnp.clip(u0 - DIL_RADIUS, 0, u_len - win), DIL_RADIUS)
        variant = (u0 - start) // DIL_RADIUS
        base = pl.multiple_of(r * u_len + start, DIL_RADIUS)

        def slab(c):
            row = (sub * n_slabs + c) * (CHUNK if dil > 1 else DIL_TU) + r * group
            return pl.ds(pl.multiple_of(row, group), group)

        for pair in range(n_pairs):
            cols = slice(pair * LANES, (pair + 1) * LANES)
            q = jnp.concatenate([q_ref[slab(c), cols] for c in range(n_slabs)], axis=0)
            zero = jnp.zeros_like(q)
            qcat = jnp.concatenate([jnp.where(lo, q, zero), jnp.where(lo, zero, q)], axis=0)
            st = _dot_nt(k_ref[pl.ds(base, win), cols], qcat) + bias_ref[pair, variant]
            m = jnp.max(st, axis=0, keepdims=True)
            p = jnp.exp2(st - m)
            rows_ref[2 * pair:2 * pair + 1, :] = m[:, :DIL_TU]
            rows_ref[2 * pair + 1:2 * pair + 2, :] = m[:, DIL_TU:]
            den = jnp.sum(p, axis=0, keepdims=True)
            rows_ref[DEN_ROW + 2 * pair:DEN_ROW + 2 * pair + 1, :] = den[:, :DIL_TU]
            rows_ref[DEN_ROW + 2 * pair + 1:DEN_ROW + 2 * pair + 2, :] = den[:, DIL_TU:]
            pb = p.astype(BF16)
            vw = v_ref[pl.ds(base, win), cols]
            o0 = _dot_tn(pb[:, :DIL_TU], vw)
            o1 = _dot_tn(pb[:, DIL_TU:], vw)
            out = jnp.where(lo, o0, o1)
            for c in range(n_slabs):
                n_ref[slab(c), cols] = out[c * group:(c + 1) * group, :].astype(n_ref.dtype)
        stats = rows_ref[...].T
        pieces = [stats] if dil == 1 else _split3(stats)
        for n, piece in enumerate(pieces):
            cols = slice(b_w + n * LANES, b_w + (n + 1) * LANES)
            for c in range(n_slabs):
                n_ref[slab(c), cols] = piece[c * group:(c + 1) * group, :].astype(n_ref.dtype)

    def tiles(it, carry):
        for n in range(per_iter):
            tile(it * per_iter + n, rows_ref.at[n])
        return carry

    lax.fori_loop(0, n_sub * dil // per_iter, tiles, 0)

    if dil == 1:
        o_ref[...] = n_ref[:, :b_w].astype(o_ref.dtype)
        st_ref[...] = n_ref[:, b_w:]
    else:
        for c in range(o_ref.shape[0] // CHUNK):
            rows = slice(c * CHUNK, (c + 1) * CHUNK)
            nat = _dot(inv_ref[...], n_ref[rows, :])
            o_ref[rows, :] = nat[:, :b_w].astype(o_ref.dtype)
            st_ref[rows, :] = (nat[:, b_w:b_w + LANES] + nat[:, b_w + LANES:b_w + 2 * LANES]
                               + nat[:, b_w + 2 * LANES:])


def _dilated_bias(slopes, dil):
    win = DIL_TU + 2 * DIL_RADIUS
    r = np.arange(win)[:, None]
    c = np.arange(DIL_TU)[None, :]
    du = np.stack([np.abs(c + off - r) for off in (0, DIL_RADIUS, 2 * DIL_RADIUS)])
    alibi = -(du * dil).astype(np.float32)[None] * slopes[:, None, None, None]
    table = np.where(du[None] <= DIL_RADIUS, alibi, np.float32(NEG_BIG))
    table = table.reshape(len(slopes) // 2, 2, 3, win, DIL_TU).transpose(0, 2, 3, 1, 4)
    return jnp.asarray(table.reshape(len(slopes) // 2, 3, win, 2 * DIL_TU), F32)


def _dilated(qc, kg, vg, slopes, dil):
    bn, seq, b_w = qc.shape
    u_len = seq // dil
    n_sub = max(1, DIL_TOKENS // (DIL_TU * dil))
    tl = n_sub * DIL_TU * dil
    per_iter = 16
    extra = LANES if dil == 1 else 3 * LANES
    n_dtype = F32 if dil == 1 else BF16
    bias = _dilated_bias(slopes, dil)
    inv = jnp.asarray(_chunk_perm(dil).T, BF16)
    o, st = pl.pallas_call(
        functools.partial(_dilated_kernel, dil=dil, u_len=u_len, n_sub=n_sub, per_iter=per_iter),
        out_shape=[jax.ShapeDtypeStruct((bn, seq, b_w), BF16),
                   jax.ShapeDtypeStruct((bn, seq, LANES), F32)],
        grid=(bn, seq // tl),
        in_specs=[_resident(bias.shape), _resident(inv.shape),
                  pl.BlockSpec((None, tl, b_w), lambda b, i: (b, i, 0)),
                  pl.BlockSpec((None, seq, b_w), lambda b, i: (b, 0, 0), pipeline_mode=pl.Buffered(1)),
                  pl.BlockSpec((None, seq, b_w), lambda b, i: (b, 0, 0), pipeline_mode=pl.Buffered(1))],
        out_specs=[pl.BlockSpec((None, tl, b_w), lambda b, i: (b, i, 0)),
                   pl.BlockSpec((None, tl, LANES), lambda b, i: (b, i, 0))],
        scratch_shapes=[pltpu.VMEM((tl, b_w + extra), n_dtype),
                        pltpu.VMEM((per_iter, LANES, DIL_TU), F32)],
        compiler_params=_params("parallel", "parallel"),
        name=f"dilated{dil}",
    )(bias, inv, qc, kg, vg)
    return o.reshape(bn * seq, b_w), st.reshape(bn * seq, LANES)


def _conv_kernel(prev_ref, cur_ref, next_ref, w_ref, b_ref, g_ref, o_ref, pad_ref, cv_ref, *, ts, tr):
    i = pl.program_id(1)
    n = pl.num_programs(1)
    prev = prev_ref[...]
    nxt = next_ref[...]
    pad_ref[0:CONV_HALO, :] = jnp.where(i > 0, prev, jnp.zeros_like(prev))
    pad_ref[CONV_HALO:CONV_HALO + ts, :] = cur_ref[...]
    pad_ref[CONV_HALO + ts:, :] = jnp.where(i < n - 1, nxt, jnp.zeros_like(nxt))
    base = CONV_HALO - CONV_W // 2
    wrows = tr + 2 * CONV_HALO

    def rows(r, carry):
        t0 = pl.multiple_of(r * tr, tr)
        for c in range(o_ref.shape[1] // LANES):
            cols = slice(c * LANES, (c + 1) * LANES)
            win = pad_ref[pl.ds(t0, wrows), cols]
            acc = jnp.zeros((tr, LANES), F32) + b_ref[:, cols]
            for b in range(SUBLANES):
                wb = win if b == 0 else pltpu.roll(win, shift=wrows - b, axis=0)
                for a in range(2 * CONV_HALO // SUBLANES):
                    tap = SUBLANES * a + b - base
                    if 0 <= tap < CONV_W:
                        acc = acc + wb[SUBLANES * a:SUBLANES * a + tr, :] * w_ref[tap:tap + 1, cols]
            cv_ref[pl.ds(t0, tr), cols] = acc
        y = _rms(cv_ref[pl.ds(t0, tr), :], g_ref[...])
        o_ref[pl.ds(t0, tr), :] = (y * jax.nn.sigmoid(y)).astype(o_ref.dtype)
        return carry

    lax.fori_loop(0, ts // tr, rows, 0)


def _conv(z, dw_w, dw_b, norm_g, *, ts=1024, tr=128):
    bn, seq, c = z.shape
    per = ts // CONV_HALO
    last = seq // CONV_HALO - 1
    return pl.pallas_call(
        functools.partial(_conv_kernel, ts=ts, tr=tr),
        out_shape=jax.ShapeDtypeStruct((bn, seq, c), BF16),
        grid=(bn, seq // ts),
        in_specs=[pl.BlockSpec((None, CONV_HALO, c), lambda b, i: (b, jnp.maximum(i * per - 1, 0), 0)),
                  pl.BlockSpec((None, ts, c), lambda b, i: (b, i, 0)),
                  pl.BlockSpec((None, CONV_HALO, c), lambda b, i: (b, jnp.minimum((i + 1) * per, last), 0)),
                  _resident(dw_w.shape), _resident((1, c)), _resident((1, c))],
        out_specs=pl.BlockSpec((None, ts, c), lambda b, i: (b, i, 0)),
        scratch_shapes=[pltpu.VMEM((ts + 2 * CONV_HALO, c), F32), pltpu.VMEM((ts, c), F32)],
        compiler_params=_params("parallel", "parallel"),
        name="conv",
    )(z, z, z, dw_w, dw_b, norm_g)


def _merge_kernel(x_ref, ya_ref, o1_ref, o2_ref, o3_ref, l1_ref, l2_ref, l3_ref, yc_ref, gate_ref,
                  exp_ref, wa_ref, wb_ref, wc_ref, wo_ref, out_ref, *, d):
    stats = (l1_ref[...], l2_ref[...], l3_ref[...])
    dens = [pltpu.roll(s, LANES - DEN_ROW, axis=1) for s in stats]
    top = jnp.maximum(jnp.maximum(stats[0], stats[1]), stats[2])
    es = [jnp.exp2(s - top) for s in stats]
    total = es[0] * dens[0] + es[1] * dens[1] + es[2] * dens[2]
    heads = lax.broadcasted_iota(jnp.int32, (1, LANES), 1) < exp_ref.shape[1] // HEAD_DIM
    inv = jnp.where(heads, 1.0 / total, 0.0)
    mix = None
    for e, o_ref in zip(es, (o1_ref, o2_ref, o3_ref)):
        w = e * inv
        wide = _dot(w.astype(BF16), exp_ref[...])
        term = wide * o_ref[...].astype(F32)
        mix = term if mix is None else mix + term
    slab = ya_ref.shape[1]
    ya = _dot_tn(ya_ref[0], wa_ref[0:slab, :])
    for n in range(1, ya_ref.shape[0]):
        ya = ya + _dot_tn(ya_ref[n], wa_ref[n * slab:(n + 1) * slab, :])
    yb = _dot(mix.astype(BF16), wb_ref[...])
    yc = _dot(yc_ref[...], wc_ref[...])
    merged = (gate_ref[:, 0:d].astype(F32) * ya + gate_ref[:, d:2 * d].astype(F32) * yb
              + gate_ref[:, 2 * d:3 * d].astype(F32) * yc)
    out_ref[...] = x_ref[...] + _dot(merged.astype(BF16), wo_ref[...])


def _merge(x2, yat, outs, lses, yc, gates, expand, wa, wb, wc, wo, *, tm=512):
    t, d = x2.shape
    row = lambda i: (i, 0)
    per_batch = yat.shape[3] // tm
    tiles = [x2, *outs, *lses, yc, gates]
    weights = [expand, wa, wb, wc, wo]
    specs = [pl.BlockSpec((tm, a.shape[1]), row) for a in tiles]
    specs.insert(1, pl.BlockSpec((None,) + yat.shape[1:3] + (tm,),
                                 lambda i: (i // per_batch, 0, 0, i % per_batch)))
    tiles.insert(1, yat)
    return pl.pallas_call(
        functools.partial(_merge_kernel, d=d),
        out_shape=jax.ShapeDtypeStruct((t, d), F32),
        grid=(t // tm,),
        in_specs=specs + [_resident(w.shape) for w in weights],
        out_specs=pl.BlockSpec((tm, d), row),
        compiler_params=_params("parallel"),
        name="merge",
    )(*tiles, *weights)


def _alibi_slopes(n):
    return (2.0 ** (-8.0 * np.arange(1, n + 1) / n) * LOG2E).astype(np.float32)


def _pair_gain(g, scale=1.0):
    return (jnp.concatenate([g, g]) * scale).reshape(1, LANES).astype(F32)


def kernel(x, ffn1_norm, ffn1_w_up, ffn1_w_down, mix_norm, w_in, b_in, a_q_norm, a_k_norm, a_lambda, a_sub_norm, w_out_a, b_q_norm, b_k_norm, w_out_b, c_dw_w, c_dw_b, c_norm, w_out_c, w_out, ffn2_norm, ffn2_w_up, ffn2_w_down):
    bn, seq, d = x.shape
    depth = w_in.shape[0]
    a_w = w_out_a.shape[1]
    b_w = w_out_b.shape[1]
    c_ch = w_out_c.shape[1]
    sizes = (a_w, a_w, a_w, b_w, b_w, b_w, 2 * c_ch, 3 * d)
    slopes_a = _alibi_slopes(A_HEADS)
    slopes_b = _alibi_slopes(B_HEADS)
    expand = (jnp.arange(LANES)[:, None] == (jnp.arange(b_w)[None, :] // HEAD_DIM)).astype(BF16)
    row = lambda v: v.reshape(1, -1).astype(F32)

    x2 = x.reshape(bn * seq, d)
    for l in range(depth):
        x2 = _ffn(x2, row(ffn1_norm[l]), ffn1_w_up[l].astype(BF16), ffn1_w_down[l].astype(BF16))

        dils = tuple(dil for _, dil in DILATED_PATTERNS if dil > 1)
        aqt, ak, av, bq, bk, bv, z, gates, *regrouped = _inproj(
            x2, row(mix_norm[l]), w_in[l].astype(BF16), row(b_in[l]),
            _pair_gain(a_q_norm[l], ATTN_SCALE * LOG2E), _pair_gain(a_k_norm[l]),
            _pair_gain(b_q_norm[l], ATTN_SCALE * LOG2E), _pair_gain(b_k_norm[l]), sizes, bn, dils)

        lam0 = 0.8 - 0.6 * math.exp(-0.3 * l)
        to3 = lambda t: t.reshape(bn, seq, t.shape[-1])
        yat = _diffattn(aqt, to3(ak), to3(av), slopes_a, a_lambda[l].astype(F32),
                        a_sub_norm[l].reshape(-1, 1).astype(F32), lam0=lam0)
        yat = yat.reshape(bn, A_HEADS // 2, 2 * LANES, seq)

        outs, lses = [], []
        for _, dil in DILATED_PATTERNS:
            qkv = (bq, bk, bv) if dil == 1 else regrouped[3 * dils.index(dil):3 * dils.index(dil) + 3]
            o, lse = _dilated(*(to3(t) for t in qkv), slopes_b, dil)
            outs.append(o)
            lses.append(lse)

        yc = _conv(to3(z), c_dw_w[l].astype(F32), row(c_dw_b[l]), row(c_norm[l])).reshape(bn * seq, c_ch)

        x2 = _merge(x2, yat, outs, lses, yc, gates, expand,
                    w_out_a[l].astype(BF16), w_out_b[l].astype(BF16), w_out_c[l].astype(BF16),
                    w_out[l].astype(BF16))

        x2 = _ffn(x2, row(ffn2_norm[l]), ffn2_w_up[l].astype(BF16), ffn2_w_down[l].astype(BF16))
    return x2.reshape(bn, seq, d)
```
